```python
import jax, jax.numpy as jnp
from jax import lax
import numpy as np

D_MODEL = 1024
BATCH = 8
SEQ = 2048
DEPTH = 1
DEC_BATCH = 128
DEC_SEQ = 8
PAST_LEN = 16384
PAGE_SIZE = 128

HEAD_DIM_A = 64
N_HEADS_A = (D_MODEL // 2) // HEAD_DIM_A
N_KV_A = 2
GROUP_A = N_HEADS_A // N_KV_A
ROT_DIM_A = HEAD_DIM_A // 4
CMP_BLOCK = 32
SEL_BLOCK = 64
N_SEL = 16
WINDOW = 512
N_HEADS_B = 8
NOPE_DIM = 64
ROPE_DIM_B = 32
V_DIM = (D_MODEL - N_HEADS_A * HEAD_DIM_A) // N_HEADS_B
Q_RANK = D_MODEL // 4
KV_RANK = D_MODEL // 8
ROPE_THETA = 500000.0
Q_BLOCK = 128
SEL_Q_BLOCK = 64
LN_EPS = 1e-5
RMS_EPS = 1e-6
WIDTH_A = N_HEADS_A * HEAD_DIM_A
WIDTH_B = N_HEADS_B * V_DIM
MIX_WIDTH = WIDTH_A + WIDTH_B
KV_W = N_KV_A * HEAD_DIM_A
SCALE_A = HEAD_DIM_A ** -0.5
MLA_SCALE = (NOPE_DIM + ROPE_DIM_B) ** -0.5
IN_SPLITS = (('q_a', WIDTH_A), ('k_cmp', KV_W), ('v_cmp', KV_W), ('k_sel', KV_W), ('v_sel', KV_W),
             ('k_win', KV_W), ('v_win', KV_W), ('gate_a', 3 * N_HEADS_A), ('z_a', WIDTH_A),
             ('c_q', Q_RANK), ('c_kv', KV_RANK), ('k_pe', ROPE_DIM_B), ('z_b', WIDTH_B))
IN_WIDTH = sum(w for _, w in IN_SPLITS)

kernel_name = 'nsa_mla_parallel_heads_decode_step'


def _layernorm(x, g, b):
    xf = x.astype(jnp.float32)
    mu = jnp.mean(xf, -1, keepdims=True)
    var = jnp.mean(jnp.square(xf - mu), -1, keepdims=True)
    return ((xf - mu) * lax.rsqrt(var + LN_EPS) * g + b).astype(x.dtype)


def _rmsnorm(x, g):
    xf = x.astype(jnp.float32)
    return (xf * lax.rsqrt(jnp.mean(xf * xf, -1, keepdims=True) + RMS_EPS) * g).astype(x.dtype)


def _rope(x, pos, rot_dim):
    half = rot_dim // 2
    inv = ROPE_THETA ** (-jnp.arange(half, dtype=jnp.float32) * 2.0 / rot_dim)
    ang = pos.astype(jnp.float32)[:, None] * inv
    cos, sin = jnp.cos(ang)[:, None, :], jnp.sin(ang)[:, None, :]
    xf = x.astype(jnp.float32)
    x1, x2 = xf[..., :half], xf[..., half:rot_dim]
    out = jnp.concatenate([x1 * cos - x2 * sin, x1 * sin + x2 * cos, xf[..., rot_dim:]], -1)
    return out.astype(x.dtype)


def _masked_softmax(s, mask):
    s = jnp.where(mask, s.astype(jnp.float32), -1e30)
    return jnp.where(mask, jax.nn.softmax(s, axis=-1), 0.0)


def _split_in(proj):
    offs = [int(o) for o in np.cumsum([w for _, w in IN_SPLITS])[:-1]]
    return dict(zip([n for n, _ in IN_SPLITS], jnp.split(proj, offs, axis=-1)))


def _project(h, pos, w_in, q_norm_g, w_uq, kv_norm_g, w_uk):
    B, T, _ = h.shape
    pr = _split_in(h @ w_in)
    kv = lambda n: pr[n].reshape(B, T, N_KV_A, HEAD_DIM_A)
    q_a = _rope(pr['q_a'].reshape(B, T, N_HEADS_A, HEAD_DIM_A), pos, ROT_DIM_A)
    q_b = jnp.einsum('btr,rhe->bthe', _rmsnorm(pr['c_q'], q_norm_g), w_uq)
    q_abs = jnp.einsum('bthd,rhd->bthr', q_b[..., :NOPE_DIM], w_uk)
    return dict(
        q_a=q_a.reshape(B, T, N_KV_A, GROUP_A, HEAD_DIM_A),
        k_cmp=kv('k_cmp'), v_cmp=kv('v_cmp'),
        k_sel=_rope(kv('k_sel'), pos, ROT_DIM_A), v_sel=kv('v_sel'),
        k_win=_rope(kv('k_win'), pos, ROT_DIM_A), v_win=kv('v_win'),
        gate_a=jax.nn.sigmoid(pr['gate_a']).reshape(B, T, N_HEADS_A, 3),
        z_a=pr['z_a'], z_b=pr['z_b'],
        q_abs=q_abs, q_pe=_rope(q_b[..., NOPE_DIM:], pos, ROPE_DIM_B),
        c_kv=_rmsnorm(pr['c_kv'], kv_norm_g),
        k_pe=_rope(pr['k_pe'][:, :, None, :], pos, ROPE_DIM_B)[:, :, 0, :],
    )


def _compress(k_mean, v_mean, w_phi_k, w_phi_v):
    nc = k_mean.shape[1]
    kc = jnp.einsum('bnkd,kde->bnke', k_mean, w_phi_k)
    kc = _rope(kc, jnp.arange(nc, dtype=jnp.int32) * CMP_BLOCK + (CMP_BLOCK - 1), ROT_DIM_A)
    vc = jnp.einsum('bnkd,kde->bnke', v_mean, w_phi_v)
    return kc, vc


def _cmp_branch(q, pos, kc, vc):
    nc = kc.shape[1]
    s = jnp.einsum('btkgd,bnkd->bkgtn', q, kc) * SCALE_A
    blk_end = jnp.arange(nc) * CMP_BLOCK + CMP_BLOCK - 1
    p = _masked_softmax(s, blk_end[None, :] <= pos[:, None])
    o = jnp.einsum('bkgtn,bnkd->btkgd', p.astype(vc.dtype), vc)
    return o, p.sum(axis=2)


def _select_blocks(imp, pos, n_blocks):
    B, KV, T, nc = imp.shape
    ratio = SEL_BLOCK // CMP_BLOCK
    imp = jnp.pad(imp, ((0, 0), (0, 0), (0, 0), (0, n_blocks * ratio - nc)))
    score = imp.reshape(B, KV, T, n_blocks, ratio).sum(-1)
    j = jnp.arange(n_blocks)
    cur = (pos // SEL_BLOCK)[:, None]
    forced = (j == 0) | (j == cur) | (j == cur - 1)
    score = jnp.where(j <= cur, jnp.where(forced, jnp.inf, score), -jnp.inf)
    _, idx = lax.top_k(score, min(N_SEL, n_blocks))
    return idx, idx <= cur


def _sel_prompt(q, pos, idx, valid, k_sel, v_sel):
    B, T = q.shape[:2]
    n_top = idx.shape[-1]
    kb = k_sel.reshape(B, T // SEL_BLOCK, SEL_BLOCK, N_KV_A, HEAD_DIM_A)
    vb = v_sel.reshape(B, T // SEL_BLOCK, SEL_BLOCK, N_KV_A, HEAD_DIM_A)
    b_i = jnp.arange(B)[:, None, None, None]
    h_i = jnp.arange(N_KV_A)[None, :, None, None]
    nq = T // SEL_Q_BLOCK

    def block(args):
        qb, ib, mb, pb = args
        ks = kb[b_i, ib, :, h_i]
        vs = vb[b_i, ib, :, h_i]
        kpos = ib[..., None] * SEL_BLOCK + jnp.arange(SEL_BLOCK)
        mask = (mb[..., None] & (kpos <= pb[:, None, None])).reshape(B, N_KV_A, SEL_Q_BLOCK, n_top * SEL_BLOCK)
        s = jnp.einsum('bqkgd,bkqnsd->bkqgns', qb, ks) * SCALE_A
        s = s.reshape(B, N_KV_A, SEL_Q_BLOCK, GROUP_A, n_top * SEL_BLOCK)
        p = _masked_softmax(s, mask[:, :, :, None, :])
        p = p.reshape(B, N_KV_A, SEL_Q_BLOCK, GROUP_A, n_top, SEL_BLOCK).astype(vs.dtype)
        return jnp.einsum('bkqgns,bkqnsd->bqkgd', p, vs)

    qs = q.reshape(B, nq, SEL_Q_BLOCK, N_KV_A, GROUP_A, HEAD_DIM_A).swapaxes(0, 1)
    blk = lambda a: a.reshape(B, N_KV_A, nq, SEL_Q_BLOCK, n_top).transpose(2, 0, 1, 3, 4)
    o = lax.map(block, (qs, blk(idx), blk(valid), pos.reshape(nq, SEL_Q_BLOCK)))
    return o.swapaxes(0, 1).reshape(B, T, N_KV_A, GROUP_A, HEAD_DIM_A)


def _sel_sample(q, pos, idx, valid, pool_k, pool_v, layer, page_table, k_new, v_new):
    DB, DS = q.shape[:2]
    n_top = idx.shape[-1]
    nsp = PAST_LEN // SEL_BLOCK
    from_past = valid & (idx < nsp)
    tok = jnp.minimum(idx, nsp - 1)[..., None] * SEL_BLOCK + jnp.arange(SEL_BLOCK)
    b_i = jnp.arange(DB)[:, None, None, None, None]
    h_i = jnp.arange(N_KV_A)[None, :, None, None, None]
    phys = page_table[b_i, tok // PAGE_SIZE]
    off = tok % PAGE_SIZE
    ks = pool_k[layer, phys, off, h_i]
    vs = pool_v[layer, phys, off, h_i]
    new_pos = PAST_LEN + jnp.arange(DS)
    in_sel = jnp.any(idx[..., None] == (new_pos // SEL_BLOCK), axis=3)
    m_new = in_sel & (new_pos[None, :] <= pos[:, None])
    m_past = jnp.broadcast_to(from_past[..., None], tok.shape).reshape(DB, N_KV_A, DS, n_top * SEL_BLOCK)
    s_past = jnp.einsum('btkgd,bktnsd->bktgns', q, ks).reshape(DB, N_KV_A, DS, GROUP_A, n_top * SEL_BLOCK)
    s_new = jnp.einsum('btkgd,bskd->bktgs', q, k_new)
    s = jnp.concatenate([s_past, s_new], -1) * SCALE_A
    p = _masked_softmax(s, jnp.concatenate([m_past, m_new], -1)[:, :, :, None, :])
    n_p = n_top * SEL_BLOCK
    p_past = p[..., :n_p].reshape(DB, N_KV_A, DS, GROUP_A, n_top, SEL_BLOCK).astype(vs.dtype)
    return (jnp.einsum('bktgns,bktnsd->btkgd', p_past, vs)
            + jnp.einsum('bktgs,bskd->btkgd', p[..., n_p:].astype(v_new.dtype), v_new))


def _win_prompt(q, pos, k, v):
    B, T = q.shape[:2]
    nq, nw = T // Q_BLOCK, WINDOW // Q_BLOCK
    pad = ((0, 0), (WINDOW, 0), (0, 0), (0, 0))

    def band(x):
        xb = jnp.pad(x, pad).reshape(B, nq + nw, Q_BLOCK, N_KV_A, HEAD_DIM_A)
        return jnp.concatenate([xb[:, i:i + nq] for i in range(nw + 1)], axis=2)

    kband, vband = band(k), band(v)
    qp = pos.reshape(nq, Q_BLOCK)
    kp = qp[:, :1] - WINDOW + jnp.arange(Q_BLOCK + WINDOW)
    kq, qq = kp[:, None, :], qp[:, :, None]
    mask = (kq >= 0) & (kq <= qq) & (kq > qq - WINDOW)
    s = jnp.einsum('bnqkgd,bnskd->bnkgqs', q.reshape(B, nq, Q_BLOCK, N_KV_A, GROUP_A, HEAD_DIM_A), kband) * SCALE_A
    p = _masked_softmax(s, mask[None, :, None, None])
    o = jnp.einsum('bnkgqs,bnskd->bnqkgd', p.astype(v.dtype), vband)
    return o.reshape(B, T, N_KV_A, GROUP_A, HEAD_DIM_A)


def _win_sample(q, pos, buf_k, buf_v, k_new, v_new):
    wb = buf_k.shape[1]
    k = jnp.concatenate([buf_k, k_new], 1)
    v = jnp.concatenate([buf_v, v_new], 1)
    kp = PAST_LEN - wb + jnp.arange(k.shape[1])
    mask = (kp[None, :] <= pos[:, None]) & (kp[None, :] > pos[:, None] - WINDOW)
    s = jnp.einsum('btkgd,bskd->bkgts', q, k) * SCALE_A
    p = _masked_softmax(s, mask)
    o = jnp.einsum('bkgts,bskd->btkgd', p.astype(v.dtype), v)
    return o, k[:, -wb:], v[:, -wb:]


def _gate_combine(gate, o_cmp, o_sel, o_win):
    B, T = gate.shape[:2]
    hd = lambda o: o.reshape(B, T, N_HEADS_A, HEAD_DIM_A)
    o = gate[..., 0:1] * hd(o_cmp) + gate[..., 1:2] * hd(o_sel) + gate[..., 2:3] * hd(o_win)
    return o.reshape(B, T, WIDTH_A)


def _nsa_prompt(pr, pos, w_phi_k, w_phi_v):
    q = pr['q_a']
    B, T = q.shape[:2]
    bmean = lambda x: jnp.mean(x.reshape(B, T // CMP_BLOCK, CMP_BLOCK, N_KV_A, HEAD_DIM_A), axis=2)
    kc, vc = _compress(bmean(pr['k_cmp']), bmean(pr['v_cmp']), w_phi_k, w_phi_v)
    o_cmp, imp = _cmp_branch(q, pos, kc, vc)
    idx, valid = _select_blocks(imp, pos, T // SEL_BLOCK)
    o_sel = _sel_prompt(q, pos, idx, valid, pr['k_sel'], pr['v_sel'])
    o_win = _win_prompt(q, pos, pr['k_win'], pr['v_win'])
    return _gate_combine(pr['gate_a'], o_cmp, o_sel, o_win)


def _nsa_sample(ps, pos, pool_k_cmp, pool_v_cmp, pool_k_sel, pool_v_sel, win_k, win_v, layer, page_table, w_phi_k, w_phi_v):
    q = ps['q_a']
    DB, DS = q.shape[:2]
    ncp = PAST_LEN // CMP_BLOCK
    tok = jnp.arange(ncp)[:, None] * CMP_BLOCK + jnp.arange(CMP_BLOCK)[None, :]
    phys = page_table[:, tok // PAGE_SIZE]
    off = tok % PAGE_SIZE
    past_mean = lambda pool: jnp.mean(pool[layer, phys, off], axis=2)
    k_mean, v_mean = past_mean(pool_k_cmp), past_mean(pool_v_cmp)
    n_new_c = DS // CMP_BLOCK
    if n_new_c > 0:
        new_mean = lambda x: jnp.mean(x[:, :n_new_c * CMP_BLOCK].reshape(DB, n_new_c, CMP_BLOCK, N_KV_A, HEAD_DIM_A), axis=2)
        k_mean = jnp.concatenate([k_mean, new_mean(ps['k_cmp'])], 1)
        v_mean = jnp.concatenate([v_mean, new_mean(ps['v_cmp'])], 1)
    kc, vc = _compress(k_mean, v_mean, w_phi_k, w_phi_v)
    o_cmp, imp = _cmp_branch(q, pos, kc, vc)
    n_blocks = PAST_LEN // SEL_BLOCK + (DS + SEL_BLOCK - 1) // SEL_BLOCK
    idx, valid = _select_blocks(imp, pos, n_blocks)
    o_sel = _sel_sample(q, pos, idx, valid, pool_k_sel, pool_v_sel, layer, page_table, ps['k_sel'], ps['v_sel'])
    o_win, new_wk, new_wv = _win_sample(q, pos, win_k, win_v, ps['k_win'], ps['v_win'])
    return _gate_combine(ps['gate_a'], o_cmp, o_sel, o_win), new_wk, new_wv


def _mla_scores(q_abs, q_pe, ckv, kpe):
    return (jnp.einsum('bthr,bsr->bhts', q_abs, ckv) + jnp.einsum('bthe,bse->bhts', q_pe, kpe)) * MLA_SCALE


def _mla_prompt(pr, pos, w_uv):
    q_abs, q_pe, ckv, kpe = pr['q_abs'], pr['q_pe'], pr['c_kv'], pr['k_pe']
    B, T = q_abs.shape[:2]
    nq = T // Q_BLOCK

    def block(args):
        qa, qp, pb = args
        p = _masked_softmax(_mla_scores(qa, qp, ckv, kpe), pos[None, :] <= pb[:, None])
        return jnp.einsum('bhts,bsr->bthr', p.astype(ckv.dtype), ckv)

    blocks = lambda x: x.reshape((B, nq, Q_BLOCK) + x.shape[2:]).swapaxes(0, 1)
    o_lat = lax.map(block, (blocks(q_abs), blocks(q_pe), pos.reshape(nq, Q_BLOCK)))
    o_lat = o_lat.swapaxes(0, 1).reshape(B, T, N_HEADS_B, KV_RANK)
    return jnp.einsum('bthr,rhd->bthd', o_lat, w_uv).reshape(B, T, WIDTH_B)


def _mla_sample(ps, pos, pool_ckv, pool_kpe, layer, page_table, w_uv):
    q_abs, q_pe, ckv_new, kpe_new = ps['q_abs'], ps['q_pe'], ps['c_kv'], ps['k_pe']
    DB, DS = q_abs.shape[:2]
    tok = jnp.arange(PAST_LEN)
    phys = page_table[:, tok // PAGE_SIZE]
    off = tok % PAGE_SIZE
    ckv_past = pool_ckv[layer, phys, off]
    kpe_past = pool_kpe[layer, phys, off]
    s = jnp.concatenate([_mla_scores(q_abs, q_pe, ckv_past, kpe_past),
                         _mla_scores(q_abs, q_pe, ckv_new, kpe_new)], -1)
    kpos = jnp.concatenate([tok, PAST_LEN + jnp.arange(DS)])
    p = _masked_softmax(s, kpos[None, :] <= pos[:, None]).astype(ckv_new.dtype)
    o_lat = (jnp.einsum('bhts,bsr->bthr', p[..., :PAST_LEN], ckv_past)
             + jnp.einsum('bhts,bsr->bthr', p[..., PAST_LEN:], ckv_new))
    return jnp.einsum('bthr,rhd->bthd', o_lat, w_uv).reshape(DB, DS, WIDTH_B)


def _residual_out(h, o_a, o_b, pr, w_o, ln_g, ln_b, alpha):
    mixed = jnp.concatenate([o_a * jax.nn.silu(pr['z_a']), o_b * jax.nn.silu(pr['z_b'])], -1)
    return _layernorm(alpha * h + mixed @ w_o, ln_g, ln_b)


def setup_inputs(seed: int = 0) -> dict:
    key = jax.random.key(seed)
    ks = jax.random.split(key, 24)
    n_pages = PAST_LEN // PAGE_SIZE
    n_used = DEC_BATCH * n_pages
    n_phys = n_used + max(1, n_used // 4)
    wb = min(WINDOW, PAST_LEN)
    nrm = lambda k, shape, scale=1.0: jax.random.normal(k, shape, jnp.float32) * scale
    gain = lambda k, shape: 1.0 + 0.1 * jax.random.normal(k, shape, jnp.float32)
    kv_pool = (DEPTH, n_phys, PAGE_SIZE, N_KV_A, HEAD_DIM_A)
    win = (DEPTH, DEC_BATCH, wb, N_KV_A, HEAD_DIM_A)
    page_table = jax.random.permutation(ks[10], n_phys)[:n_used].reshape(DEC_BATCH, n_pages).astype(jnp.int32)
    beta = (8.0 * DEPTH) ** -0.25
    return {
        'x_prompt': nrm(ks[0], (BATCH, SEQ, D_MODEL)),
        'x_sample': nrm(ks[1], (DEC_BATCH, DEC_SEQ, D_MODEL)),
        'cache_k_cmp': nrm(ks[2], kv_pool),
        'cache_v_cmp': nrm(ks[3], kv_pool),
        'cache_k_sel': nrm(ks[4], kv_pool),
        'cache_v_sel': nrm(ks[5], kv_pool),
        'cache_ckv': nrm(ks[6], (DEPTH, n_phys, PAGE_SIZE, KV_RANK)),
        'cache_kpe': nrm(ks[7], (DEPTH, n_phys, PAGE_SIZE, ROPE_DIM_B)),
        'state_win_k': nrm(ks[8], win),
        'state_win_v': nrm(ks[9], win),
        'page_table': page_table,
        'ln_in_g': gain(ks[11], (D_MODEL,)),
        'ln_in_b': nrm(ks[12], (D_MODEL,), 0.02),
        'w_in': nrm(ks[13], (DEPTH, D_MODEL, IN_WIDTH), D_MODEL ** -0.5),
        'w_phi_k': nrm(ks[14], (DEPTH, N_KV_A, HEAD_DIM_A, HEAD_DIM_A), HEAD_DIM_A ** -0.5),
        'w_phi_v': nrm(ks[15], (DEPTH, N_KV_A, HEAD_DIM_A, HEAD_DIM_A), HEAD_DIM_A ** -0.5),
        'q_norm_g': gain(ks[16], (DEPTH, Q_RANK)),
        'w_uq': nrm(ks[17], (DEPTH, Q_RANK, N_HEADS_B, NOPE_DIM + ROPE_DIM_B), Q_RANK ** -0.5),
        'kv_norm_g': gain(ks[18], (DEPTH, KV_RANK)),
        'w_uk': nrm(ks[19], (DEPTH, KV_RANK, N_HEADS_B, NOPE_DIM), KV_RANK ** -0.5),
        'w_uv': nrm(ks[20], (DEPTH, KV_RANK, N_HEADS_B, V_DIM), KV_RANK ** -0.5),
        'w_o': nrm(ks[21], (DEPTH, MIX_WIDTH, D_MODEL), MIX_WIDTH ** -0.5 * beta),
        'ln_g': gain(ks[22], (DEPTH, D_MODEL)),
        'ln_b': nrm(ks[23], (DEPTH, D_MODEL), 0.02),
    }


def reference(x_prompt, x_sample, cache_k_cmp, cache_v_cmp, cache_k_sel, cache_v_sel, cache_ckv, cache_kpe,
              state_win_k, state_win_v, page_table, ln_in_g, ln_in_b, w_in, w_phi_k, w_phi_v, q_norm_g, w_uq,
              kv_norm_g, w_uk, w_uv, w_o, ln_g, ln_b):
    alpha = (2.0 * DEPTH) ** 0.25
    pos_p = jnp.arange(x_prompt.shape[1], dtype=jnp.int32)
    pos_s = PAST_LEN + jnp.arange(x_sample.shape[1], dtype=jnp.int32)
    wlen_p = min(WINDOW, x_prompt.shape[1])
    hp = _layernorm(x_prompt, ln_in_g, ln_in_b)
    hs = _layernorm(x_sample, ln_in_g, ln_in_b)
    rows = ('k_cmp', 'v_cmp', 'k_sel', 'v_sel', 'c_kv', 'k_pe')
    new_p = {n: [] for n in rows + ('win_k', 'win_v')}
    new_s = {n: [] for n in rows + ('win_k', 'win_v')}
    for l in range(DEPTH):
        pp = _project(hp, pos_p, w_in[l], q_norm_g[l], w_uq[l], kv_norm_g[l], w_uk[l])
        o_a = _nsa_prompt(pp, pos_p, w_phi_k[l], w_phi_v[l])
        o_b = _mla_prompt(pp, pos_p, w_uv[l])
        for n in rows:
            new_p[n].append(pp[n])
        new_p['win_k'].append(pp['k_win'][:, -wlen_p:])
        new_p['win_v'].append(pp['v_win'][:, -wlen_p:])
        hp = _residual_out(hp, o_a, o_b, pp, w_o[l], ln_g[l], ln_b[l], alpha)
        ps = _project(hs, pos_s, w_in[l], q_norm_g[l], w_uq[l], kv_norm_g[l], w_uk[l])
        o_a_s, wk_s, wv_s = _nsa_sample(ps, pos_s, cache_k_cmp, cache_v_cmp, cache_k_sel, cache_v_sel,
                                         state_win_k[l], state_win_v[l], l, page_table, w_phi_k[l], w_phi_v[l])
        o_b_s = _mla_sample(ps, pos_s, cache_ckv, cache_kpe, l, page_table, w_uv[l])
        for n in rows:
            new_s[n].append(ps[n])
        new_s['win_k'].append(wk_s)
        new_s['win_v'].append(wv_s)
        hs = _residual_out(hs, o_a_s, o_b_s, ps, w_o[l], ln_g[l], ln_b[l], alpha)
    st = lambda d, n: jnp.stack(d[n], axis=0)
    return (hp, hs,
            st(new_p, 'k_cmp'), st(new_p, 'v_cmp'), st(new_p, 'k_sel'), st(new_p, 'v_sel'),
            st(new_p, 'c_kv'), st(new_p, 'k_pe'), st(new_p, 'win_k'), st(new_p, 'win_v'),
            st(new_s, 'k_cmp'), st(new_s, 'v_cmp'), st(new_s, 'k_sel'), st(new_s, 'v_sel'),
            st(new_s, 'c_kv'), st(new_s, 'k_pe'), st(new_s, 'win_k'), st(new_s, 'win_v'))
```

```python
import functools

import numpy as np
import jax
import jax.numpy as jnp
from jax import lax
from jax.experimental import pallas as pl
from jax.experimental.pallas import tpu as pltpu

D_MODEL = 1024
HEAD_DIM_A = 64
N_HEADS_A = 8
N_KV_A = 2
GROUP_A = 4
ROT_DIM_A = 16
CMP_BLOCK = 32
SEL_BLOCK = 64
N_SEL = 16
WINDOW = 512
N_HEADS_B = 8
NOPE_DIM = 64
ROPE_DIM_B = 32
V_DIM = 64
Q_RANK = 256
KV_RANK = 128
ROPE_THETA = 500000.0
LN_EPS = 1e-5
RMS_EPS = 1e-6
PAGE_SIZE = 128
WIDTH_A = 512
WIDTH_B = 512
KV_W = 128
SCALE_A = HEAD_DIM_A ** -0.5
MLA_SCALE = (NOPE_DIM + ROPE_DIM_B) ** -0.5
IN_WIDTH = 2744

LANES = 128
NEG = -1e30

OFF_QA, OFF_KV, OFF_GATE, OFF_ZA, OFF_CQ, OFF_CKV, OFF_KPE, OFF_ZB = 0, 512, 1280, 1304, 1816, 2072, 2200, 2232

SEG_QA, SEG_KV, SEG_G, SEG_ZA, SEG_CQ, SEG_CKV, SEG_KPE, SEG_ZB = 0, 512, 1280, 2816, 3328, 3584, 3712, 3840
EXT_WIDTH = 4352

_NT = (((1,), (1,)), ((), ()))


def _paired_head(lane):
    g, s = lane // LANES, (lane % LANES) // HEAD_DIM_A
    return g + GROUP_A * s, lane % HEAD_DIM_A


def _proj_row_index():
    idx = np.full((EXT_WIDTH,), IN_WIDTH, np.int32)
    for lane in range(WIDTH_A):
        h, d = _paired_head(lane)
        idx[SEG_QA + lane] = OFF_QA + h * HEAD_DIM_A + d
        idx[SEG_ZA + lane] = OFF_ZA + h * HEAD_DIM_A + d
        for j in range(3):
            idx[SEG_G + j * WIDTH_A + lane] = OFF_GATE + h * 3 + j
    idx[SEG_KV:SEG_KV + 6 * KV_W] = OFF_KV + np.arange(6 * KV_W)
    idx[SEG_CQ:SEG_CQ + Q_RANK] = OFF_CQ + np.arange(Q_RANK)
    idx[SEG_CKV:SEG_CKV + KV_RANK] = OFF_CKV + np.arange(KV_RANK)
    idx[SEG_KPE:SEG_KPE + ROPE_DIM_B] = OFF_KPE + np.arange(ROPE_DIM_B)
    idx[SEG_ZB:SEG_ZB + WIDTH_B] = OFF_ZB + np.arange(WIDTH_B)
    return idx


def _rope_tables(pos, rot_dim, period):
    half = rot_dim // 2
    inv = ROPE_THETA ** (-jnp.arange(half, dtype=jnp.float32) * 2.0 / rot_dim)
    ang = pos.astype(jnp.float32)[:, None] * inv
    cos, sin = jnp.cos(ang), jnp.sin(ang)
    n = pos.shape[0]
    pad = jnp.zeros((n, period - rot_dim), jnp.float32)
    c = jnp.concatenate([cos, cos, pad + 1.0], axis=1)
    s_lo = jnp.concatenate([-sin, jnp.zeros_like(sin), pad], axis=1)
    s_hi = jnp.concatenate([jnp.zeros_like(sin), sin, pad], axis=1)
    rep = LANES // period
    return tuple(jnp.tile(t, (1, rep)) for t in (c, s_lo, s_hi))


def _rope(x, c, s_lo, s_hi, half):
    w = x.shape[1]
    return x * c + pltpu.roll(x, w - half, 1) * s_lo + pltpu.roll(x, half, 1) * s_hi


def _tile_lanes(t, width):
    return t if width == LANES else jnp.concatenate([t] * (width // LANES), axis=1)


def _layernorm(x, g, b):
    mu = jnp.mean(x, axis=-1, keepdims=True)
    xc = x - mu
    var = jnp.mean(xc * xc, axis=-1, keepdims=True)
    return xc * lax.rsqrt(var + LN_EPS) * g + b


def _rmsnorm(x, g):
    return x * lax.rsqrt(jnp.mean(x * x, axis=-1, keepdims=True) + RMS_EPS) * g


def _sigmoid(x):
    return 1.0 / (1.0 + jnp.exp(-x))


def _proj_kernel(x_ref, lng_ref, lnb_ref, w_ref, qng_ref, wuq_ref, wuk_ref, kvng_ref,
                 ca_ref, sal_ref, sah_ref, cb_ref, sbl_ref, sbh_ref,
                 cc_ref, scl_ref, sch_ref, wpk_ref, wpv_ref,
                 qa_ref, kv_ref, kvb_ref, g_ref, sza_ref, szb_ref, ckv_ref, kpe_ref, kmla_ref, qmla_ref,
                 kc_ref, vc_ref, *, with_cmp):
    hn = _layernorm(x_ref[...], lng_ref[...], lnb_ref[...]).astype(jnp.bfloat16)

    def seg(lo, width):
        return lax.dot_general(hn, w_ref[lo:lo + width, :], _NT, preferred_element_type=jnp.float32)

    ca, sal, sah = ca_ref[...], sal_ref[...], sah_ref[...]
    half_a = ROT_DIM_A // 2

    qa = seg(SEG_QA, WIDTH_A)
    qa = _rope(qa, _tile_lanes(ca, WIDTH_A), _tile_lanes(sal, WIDTH_A), _tile_lanes(sah, WIDTH_A), half_a)
    qa_ref[...] = (qa * SCALE_A).astype(jnp.bfloat16)

    kv = seg(SEG_KV, 6 * KV_W)
    k_cmp, v_cmp = kv[:, 0:KV_W], kv[:, KV_W:2 * KV_W]
    k_sel = _rope(kv[:, 2 * KV_W:3 * KV_W], ca, sal, sah, half_a)
    v_sel = kv[:, 3 * KV_W:4 * KV_W]
    k_win = _rope(kv[:, 4 * KV_W:5 * KV_W], ca, sal, sah, half_a)
    v_win = kv[:, 5 * KV_W:6 * KV_W]
    kv_ref[:, 0:2 * KV_W] = kv[:, 0:2 * KV_W]
    kv_ref[:, 2 * KV_W:3 * KV_W] = k_sel
    kv_ref[:, 3 * KV_W:4 * KV_W] = v_sel
    kv_ref[:, 4 * KV_W:5 * KV_W] = k_win
    kv_ref[:, 5 * KV_W:6 * KV_W] = v_win
    kvb_ref[:, 0:KV_W] = k_sel.astype(jnp.bfloat16)
    kvb_ref[:, KV_W:2 * KV_W] = v_sel.astype(jnp.bfloat16)
    kvb_ref[:, 2 * KV_W:3 * KV_W] = k_win.astype(jnp.bfloat16)
    kvb_ref[:, 3 * KV_W:4 * KV_W] = v_win.astype(jnp.bfloat16)

    for j in range(3):
        g_ref[:, j * WIDTH_A:(j + 1) * WIDTH_A] = _sigmoid(seg(SEG_G + j * WIDTH_A, WIDTH_A))

    za = seg(SEG_ZA, WIDTH_A)
    sza_ref[...] = za * _sigmoid(za)
    zb = seg(SEG_ZB, WIDTH_B)
    szb_ref[...] = zb * _sigmoid(zb)

    ckv = _rmsnorm(seg(SEG_CKV, KV_RANK), kvng_ref[...])
    ckv_ref[...] = ckv
    half_b = ROPE_DIM_B // 2
    cb, sbl, sbh = cb_ref[...], sbl_ref[...], sbh_ref[...]
    kpe = _rope(seg(SEG_KPE, LANES), cb, sbl, sbh, half_b)
    kpe_ref[...] = kpe
    kmla_ref[:, 0:KV_RANK] = ckv.astype(jnp.bfloat16)
    kmla_ref[:, KV_RANK:2 * KV_RANK] = kpe.astype(jnp.bfloat16)

    cq = _rmsnorm(seg(SEG_CQ, Q_RANK), qng_ref[...]).astype(jnp.bfloat16)
    qb = jnp.dot(cq, wuq_ref[...], preferred_element_type=jnp.float32)
    nope_w = N_HEADS_B * NOPE_DIM
    q_abs = jnp.dot(qb[:, :nope_w].astype(jnp.bfloat16), wuk_ref[...], preferred_element_type=jnp.float32)
    pe_w = N_HEADS_B * LANES
    q_pe = _rope(qb[:, nope_w:], _tile_lanes(cb, pe_w), _tile_lanes(sbl, pe_w), _tile_lanes(sbh, pe_w), half_b)
    for h in range(N_HEADS_B):
        qmla_ref[:, 2 * h * LANES:(2 * h + 1) * LANES] = q_abs[:, h * LANES:(h + 1) * LANES].astype(jnp.bfloat16)
        qmla_ref[:, (2 * h + 1) * LANES:(2 * h + 2) * LANES] = q_pe[:, h * LANES:(h + 1) * LANES].astype(jnp.bfloat16)

    if with_cmp:
        tm = k_cmp.shape[0]
        nblk = tm // CMP_BLOCK
        k_mean = jnp.mean(k_cmp.reshape(nblk, CMP_BLOCK, KV_W), axis=1).astype(jnp.bfloat16)
        v_mean = jnp.mean(v_cmp.reshape(nblk, CMP_BLOCK, KV_W), axis=1).astype(jnp.bfloat16)
        kc = jnp.dot(k_mean, wpk_ref[...], preferred_element_type=jnp.float32)
        kc_ref[...] = _rope(kc, cc_ref[...], scl_ref[...], sch_ref[...], half_a)
        vc_ref[...] = jnp.dot(v_mean, wpv_ref[...], preferred_element_type=jnp.float32)
    else:
        kc_ref[...] = jnp.zeros_like(kc_ref)
        vc_ref[...] = jnp.zeros_like(vc_ref)


def _block_diag2(w):
    z = jnp.zeros((HEAD_DIM_A, HEAD_DIM_A), w.dtype)
    return jnp.concatenate([jnp.concatenate([w[0], z], 1), jnp.concatenate([z, w[1]], 1)], 0)


def _prep_weights(w_in, w_phi_k, w_phi_v, w_uq, w_uk, w_uv, w_o):
    wt = jnp.swapaxes(w_in, 0, 1)
    wt = jnp.concatenate([wt, jnp.zeros((1, D_MODEL), wt.dtype)], axis=0)
    w_ext = jnp.take(wt, jnp.asarray(_proj_row_index()), axis=0).astype(jnp.bfloat16)
    nope = w_uq[:, :, :NOPE_DIM].reshape(Q_RANK, N_HEADS_B * NOPE_DIM)
    pe = jnp.pad(w_uq[:, :, NOPE_DIM:], ((0, 0), (0, 0), (0, LANES - ROPE_DIM_B))).reshape(Q_RANK, N_HEADS_B * LANES)
    wuq = jnp.concatenate([nope, pe], axis=1).astype(jnp.bfloat16)
    eye = jnp.eye(N_HEADS_B, dtype=w_uk.dtype)
    wuk = jnp.einsum('rhd,hg->hdgr', w_uk, eye).reshape(N_HEADS_B * NOPE_DIM, N_HEADS_B * KV_RANK).astype(jnp.bfloat16)
    wuv = jnp.einsum('rhd,hg->hrgd', w_uv, eye).reshape(N_HEADS_B * KV_RANK, N_HEADS_B * V_DIM).astype(jnp.bfloat16)
    heads, dims = _paired_head(np.arange(WIDTH_A))
    woa = jnp.take(w_o[:WIDTH_A], jnp.asarray(heads * HEAD_DIM_A + dims), axis=0).astype(jnp.bfloat16)
    wob = w_o[WIDTH_A:].astype(jnp.bfloat16)
    wpk = _block_diag2(w_phi_k).astype(jnp.bfloat16)
    wpv = _block_diag2(w_phi_v).astype(jnp.bfloat16)
    return dict(w_ext=w_ext, wuq=wuq, wuk=wuk, wuv=wuv, woa=woa, wob=wob, wpk=wpk, wpv=wpv)


def _project(x, pos_rows, cmp_pos, wts, ln_in_g, ln_in_b, q_norm_g, kv_norm_g, *, tm, with_cmp):
    m = x.shape[0]
    n_tiles = m // tm
    ta = _rope_tables(pos_rows, ROT_DIM_A, HEAD_DIM_A)
    tb = _rope_tables(pos_rows, ROPE_DIM_B, LANES)
    tc = _rope_tables(cmp_pos, ROT_DIM_A, HEAD_DIM_A)
    tab_tiles = pos_rows.shape[0] // tm
    nblk = tm // CMP_BLOCK
    cmp_tiles = max(cmp_pos.shape[0] // nblk, 1)

    row = lambda w: pl.BlockSpec((tm, w), lambda i: (i, 0))
    full = lambda a: pl.BlockSpec(a.shape, lambda i: (0,) * a.ndim)
    tab = pl.BlockSpec((tm, LANES), lambda i: (i % tab_tiles, 0))
    ctab = pl.BlockSpec((nblk, LANES), lambda i: (i % cmp_tiles, 0))
    vec = lambda v: v.reshape(1, -1)
    args = (x, vec(ln_in_g), vec(ln_in_b), wts['w_ext'], vec(q_norm_g), wts['wuq'], wts['wuk'], vec(kv_norm_g),
            *ta, *tb, *tc, wts['wpk'], wts['wpv'])
    in_specs = [row(D_MODEL)] + [full(a) for a in args[1:8]] + [tab] * 6 + [ctab] * 3 + [full(wts['wpk']), full(wts['wpv'])]
    f32, bf16 = jnp.float32, jnp.bfloat16
    outs = [(WIDTH_A, bf16), (6 * KV_W, f32), (4 * KV_W, bf16), (3 * WIDTH_A, f32), (WIDTH_A, f32), (WIDTH_B, f32),
            (KV_RANK, f32), (LANES, f32), (2 * KV_RANK, bf16), (N_HEADS_B * 2 * LANES, bf16)]
    out_shape = [jax.ShapeDtypeStruct((m, w), dt) for w, dt in outs]
    out_specs = [row(w) for w, _ in outs]
    out_shape += [jax.ShapeDtypeStruct((n_tiles * nblk, LANES), f32)] * 2
    out_specs += [pl.BlockSpec((nblk, LANES), lambda i: (i, 0))] * 2
    names = ('qa', 'kv', 'kvb', 'gates', 'sza', 'szb', 'ckv', 'kpe', 'kmla', 'qmla', 'kc', 'vc')
    res = pl.pallas_call(
        functools.partial(_proj_kernel, with_cmp=with_cmp),
        grid=(n_tiles,), in_specs=in_specs, out_specs=out_specs, out_shape=out_shape,
        compiler_params=pltpu.CompilerParams(dimension_semantics=("arbitrary",), vmem_limit_bytes=56 * 2**20),
        name="in_proj",
    )(*args)
    return dict(zip(names, res))


def _out_kernel(x_ref, lig_ref, lib_ref, ma_ref, mb_ref, woa_ref, wob_ref, g_ref, b_ref, y_ref, *, alpha):
    h = _layernorm(x_ref[...], lig_ref[...], lib_ref[...])
    o = jnp.dot(ma_ref[...], woa_ref[...], preferred_element_type=jnp.float32)
    o = o + jnp.dot(mb_ref[...], wob_ref[...], preferred_element_type=jnp.float32)
    y_ref[...] = _layernorm(alpha * h + o, g_ref[...], b_ref[...])


def _out_project(x, mixed_a, mixed_b, wts, ln_in_g, ln_in_b, ln_g, ln_b, *, tm, alpha):
    m = x.shape[0]
    row = lambda w: pl.BlockSpec((tm, w), lambda i: (i, 0))
    full = lambda a: pl.BlockSpec(a.shape, lambda i: (0,) * a.ndim)
    vec = lambda v: v.reshape(1, -1)
    args = (x, vec(ln_in_g), vec(ln_in_b), mixed_a, mixed_b, wts['woa'], wts['wob'], vec(ln_g), vec(ln_b))
    in_specs = [row(D_MODEL), full(args[1]), full(args[2]), row(WIDTH_A), row(WIDTH_B),
                full(args[5]), full(args[6]), full(args[7]), full(args[8])]
    return pl.pallas_call(
        functools.partial(_out_kernel, alpha=alpha),
        grid=(m // tm,), in_specs=in_specs, out_specs=row(D_MODEL),
        out_shape=jax.ShapeDtypeStruct((m, D_MODEL), jnp.float32),
        compiler_params=pltpu.CompilerParams(dimension_semantics=("arbitrary",)),
        name="out_proj",
    )(*args)


def _softmax_update(s, mask, m_ref, l_ref, acc_ref, v, nt_v=False):
    m_old = m_ref[...]
    m_new = jnp.maximum(m_old, jnp.max(jnp.where(mask, s, NEG), axis=-1, keepdims=True))
    p = jnp.where(mask, jnp.exp(s - m_new), 0.0)
    alpha = jnp.exp(m_old - m_new)
    l_ref[...] = alpha * l_ref[...] + jnp.sum(p, axis=-1, keepdims=True)
    pb = p.astype(jnp.bfloat16)
    if nt_v:
        pv = lax.dot_general(pb, v, _NT, preferred_element_type=jnp.float32)
    else:
        pv = jnp.dot(pb, v, preferred_element_type=jnp.float32)
    acc_ref[...] = alpha * acc_ref[...] + pv
    m_ref[...] = m_new


def _softmax_init(m_ref, l_ref, acc_ref):
    m_ref[...] = jnp.full_like(m_ref, NEG)
    l_ref[...] = jnp.zeros_like(l_ref)
    acc_ref[...] = jnp.zeros_like(acc_ref)


def _softmax_finish(l_ref, acc_ref):
    l = l_ref[...]
    return acc_ref[...] / jnp.where(l > 0.0, l, 1.0)


def _top_blocks(score, lane, cur, n_valid):
    forced = (lane == 0) | (lane == cur) | (lane == cur - 1)
    sc = jnp.where(lane <= cur, jnp.where(forced, jnp.inf, score), -jnp.inf)
    sc = jnp.where(lane < n_valid, sc, -jnp.inf)
    w = sc.shape[1]
    cnt = jnp.zeros(sc.shape, jnp.int32)
    for r in range(1, n_valid):
        cnt += (pltpu.roll(sc, r, 1) >= sc).astype(jnp.int32)
        cnt += (pltpu.roll(sc, w - r, 1) > sc).astype(jnp.int32)
    return (cnt < N_SEL) & (lane <= cur) & (lane < n_valid)


def _nsa_prompt_kernel(qa_ref, kvb_ref, kc_ref, vc_ref, g_ref, sza_ref, out_ref,
                       m_ref, l_ref, acc_ref, *, tq, tk, seq):
    i = pl.program_id(1)
    q0 = i * tq
    rows = GROUP_A * tq
    lane = lax.broadcasted_iota(jnp.int32, (tq, LANES), 1)
    t_q = q0 + lax.broadcasted_iota(jnp.int32, (tq, LANES), 0)
    low = lane < HEAD_DIM_A
    n_sel_blocks = seq // SEL_BLOCK

    qa = qa_ref[...]
    kc = kc_ref[0].astype(jnp.bfloat16)
    vc = vc_ref[0].astype(jnp.bfloat16)
    n_half = seq // CMP_BLOCK // 2
    blk = jnp.where(lane < n_half, 2 * lane, 2 * (lane - n_half) + 1)
    cmp_mask1 = (blk * CMP_BLOCK + CMP_BLOCK - 1 <= t_q) & (lane < 2 * n_half)
    cmp_mask = jnp.concatenate([cmp_mask1] * GROUP_A, axis=0)

    kcol = lax.broadcasted_iota(jnp.int32, (tq, tk), 1)
    trow = q0 + lax.broadcasted_iota(jnp.int32, (tq, tk), 0)
    eblk = lax.broadcasted_iota(jnp.int32, (LANES, tk), 0) - lax.broadcasted_iota(jnp.int32, (LANES, tk), 1) // SEL_BLOCK

    o_cmp, o_sel, o_win = [], [], []
    for k in range(N_KV_A):
        keep = low if k == 0 else ~low
        qk = jnp.concatenate([jnp.where(keep, qa[:, g * LANES:(g + 1) * LANES], 0) for g in range(GROUP_A)], axis=0)

        s = lax.dot_general(qk, kc, _NT, preferred_element_type=jnp.float32)
        m = jnp.max(jnp.where(cmp_mask, s, NEG), axis=-1, keepdims=True)
        p = jnp.where(cmp_mask, jnp.exp(s - m), 0.0)
        l = jnp.sum(p, axis=-1, keepdims=True)
        p = p / jnp.where(l > 0.0, l, 1.0)
        o_cmp.append(jnp.dot(p.astype(jnp.bfloat16), vc, preferred_element_type=jnp.float32))
        imp = p[0:tq] + p[tq:2 * tq] + p[2 * tq:3 * tq] + p[3 * tq:4 * tq]
        score = imp + pltpu.roll(imp, LANES - n_half, 1)
        sel = _top_blocks(score, lane, t_q // SEL_BLOCK, n_sel_blocks).astype(jnp.bfloat16)

        _softmax_init(m_ref, l_ref, acc_ref)
        n_tiles = (q0 + tq + tk - 1) // tk

        def sel_step(j, carry):
            k0 = pl.multiple_of(j * tk, tk)
            kt = kvb_ref[pl.ds(k0, tk), 0:KV_W]
            vt = kvb_ref[pl.ds(k0, tk), KV_W:2 * KV_W]
            s = lax.dot_general(qk, kt, _NT, preferred_element_type=jnp.float32)
            expand = (eblk == j * (tk // SEL_BLOCK)).astype(jnp.bfloat16)
            picked = jnp.dot(sel, expand, preferred_element_type=jnp.float32) > 0.5
            mask1 = picked & (k0 + kcol <= trow)
            _softmax_update(s, jnp.concatenate([mask1] * GROUP_A, axis=0), m_ref, l_ref, acc_ref, vt)
            return carry

        lax.fori_loop(0, n_tiles, sel_step, 0)
        o_sel.append(_softmax_finish(l_ref, acc_ref))

        _softmax_init(m_ref, l_ref, acc_ref)
        j_lo = jnp.maximum(q0 - WINDOW + 1, 0) // tk

        def win_step(j, carry):
            k0 = pl.multiple_of(j * tk, tk)
            kt = kvb_ref[pl.ds(k0, tk), 2 * KV_W:3 * KV_W]
            vt = kvb_ref[pl.ds(k0, tk), 3 * KV_W:4 * KV_W]
            s = lax.dot_general(qk, kt, _NT, preferred_element_type=jnp.float32)
            kp = k0 + kcol
            mask1 = (kp <= trow) & (kp > trow - WINDOW)
            _softmax_update(s, jnp.concatenate([mask1] * GROUP_A, axis=0), m_ref, l_ref, acc_ref, vt)
            return carry

        lax.fori_loop(j_lo, n_tiles, win_step, 0)
        o_win.append(_softmax_finish(l_ref, acc_ref))

    for g in range(GROUP_A):
        r = slice(g * tq, (g + 1) * tq)
        c = slice(g * LANES, (g + 1) * LANES)
        pick = lambda o: jnp.where(low, o[0][r], o[1][r])
        o = (g_ref[:, c] * pick(o_cmp) + g_ref[:, WIDTH_A + g * LANES:WIDTH_A + (g + 1) * LANES] * pick(o_sel)
             + g_ref[:, 2 * WIDTH_A + g * LANES:2 * WIDTH_A + (g + 1) * LANES] * pick(o_win))
        out_ref[:, c] = (o * sza_ref[:, c]).astype(jnp.bfloat16)


def _permute_cmp_blocks(x, b, n):
    x = x.reshape(b, n // 2, 2, LANES)
    x = jnp.concatenate([x[:, :, 0], x[:, :, 1]], axis=1)
    return jnp.pad(x, ((0, 0), (0, LANES - n), (0, 0)))


def _nsa_prompt(pr, b, seq, *, tq=128, tk=256):
    n_cmp = seq // CMP_BLOCK
    kc = _permute_cmp_blocks(pr['kc'], b, n_cmp)
    vc = _permute_cmp_blocks(pr['vc'], b, n_cmp)
    nq = seq // tq
    row = lambda w: pl.BlockSpec((tq, w), lambda bi, i: (bi * nq + i, 0))
    per_b = lambda w: pl.BlockSpec((seq, w), lambda bi, i: (bi, 0))
    cmp_spec = pl.BlockSpec((1, LANES, LANES), lambda bi, i: (bi, 0, 0))
    rows = GROUP_A * tq
    return pl.pallas_call(
        functools.partial(_nsa_prompt_kernel, tq=tq, tk=tk, seq=seq),
        grid=(b, nq),
        in_specs=[row(WIDTH_A), per_b(4 * KV_W), cmp_spec, cmp_spec, row(3 * WIDTH_A), row(WIDTH_A)],
        out_specs=row(WIDTH_A),
        out_shape=jax.ShapeDtypeStruct((b * seq, WIDTH_A), jnp.bfloat16),
        scratch_shapes=[pltpu.VMEM((rows, 1), jnp.float32), pltpu.VMEM((rows, 1), jnp.float32),
                        pltpu.VMEM((rows, LANES), jnp.float32)],
        compiler_params=pltpu.CompilerParams(dimension_semantics=("arbitrary", "arbitrary")),
        name="nsa_prompt",
    )(pr['qa'], pr['kvb'], kc, vc, pr['gates'], pr['sza'])


def _mla_prompt_kernel(q_ref, k_ref, szb_ref, wuv_ref, out_ref, m_ref, l_ref, acc_ref, olat_ref, *, tq, tk, hg):
    i = pl.program_id(1)
    q0 = i * tq
    kcol = lax.broadcasted_iota(jnp.int32, (tq, tk), 1)
    trow = q0 + lax.broadcasted_iota(jnp.int32, (tq, tk), 0)
    n_tiles = (q0 + tq + tk - 1) // tk
    qw = 2 * LANES
    for grp in range(N_HEADS_B // hg):
        q = jnp.concatenate([q_ref[:, (grp * hg + h) * qw:(grp * hg + h + 1) * qw] for h in range(hg)], axis=0)
        _softmax_init(m_ref, l_ref, acc_ref)

        def step(j, carry):
            k0 = pl.multiple_of(j * tk, tk)
            kt = k_ref[pl.ds(k0, tk), :]
            s = lax.dot_general(q, kt, _NT, preferred_element_type=jnp.float32) * MLA_SCALE
            mask1 = k0 + kcol <= trow
            _softmax_update(s, jnp.concatenate([mask1] * hg, axis=0), m_ref, l_ref, acc_ref, kt[:, 0:KV_RANK])
            return carry

        lax.fori_loop(0, n_tiles, step, 0)
        o = _softmax_finish(l_ref, acc_ref)
        for h in range(hg):
            c = (grp * hg + h) * KV_RANK
            olat_ref[:, c:c + KV_RANK] = o[h * tq:(h + 1) * tq].astype(jnp.bfloat16)
    ob = jnp.dot(olat_ref[...], wuv_ref[...], preferred_element_type=jnp.float32)
    out_ref[...] = (ob * szb_ref[...]).astype(jnp.bfloat16)


def _mla_prompt(pr, wts, b, seq, *, tq=128, tk=256, hg=4):
    nq = seq // tq
    row = lambda w: pl.BlockSpec((tq, w), lambda bi, i: (bi * nq + i, 0))
    per_b = lambda w: pl.BlockSpec((seq, w), lambda bi, i: (bi, 0))
    rows = hg * tq
    return pl.pallas_call(
        functools.partial(_mla_prompt_kernel, tq=tq, tk=tk, hg=hg),
        grid=(b, nq),
        in_specs=[row(N_HEADS_B * 2 * LANES), per_b(2 * KV_RANK), row(WIDTH_B),
                  pl.BlockSpec(wts['wuv'].shape, lambda bi, i: (0, 0))],
        out_specs=row(WIDTH_B),
        out_shape=jax.ShapeDtypeStruct((b * seq, WIDTH_B), jnp.bfloat16),
        scratch_shapes=[pltpu.VMEM((rows, 1), jnp.float32), pltpu.VMEM((rows, 1), jnp.float32),
                        pltpu.VMEM((rows, KV_RANK), jnp.float32), pltpu.VMEM((tq, N_HEADS_B * KV_RANK), jnp.bfloat16)],
        compiler_params=pltpu.CompilerParams(dimension_semantics=("arbitrary", "arbitrary")),
        name="mla_prompt",
    )(pr['qmla'], pr['kmla'], pr['szb'], wts['wuv'])


PAGES_PER_STEP = 16


def _page_specs(shape, n):
    def spec(i):
        return pl.BlockSpec((1,) + shape, lambda b, c, pt: (pt[b, c * n + i], 0, 0))
    return [spec(i) for i in range(n)]


def _cmp_pool_kernel(pt_ref, *refs, n):
    k_refs, v_refs = refs[:n], refs[n:2 * n]
    cc_ref, scl_ref, sch_ref, wpk_ref, wpv_ref, kc_ref, vc_ref = refs[2 * n:]
    blocks = PAGE_SIZE // CMP_BLOCK

    def means(page_refs):
        rows = []
        for r in page_refs:
            xt = r[0].T
            rows.append(jnp.mean(xt.reshape(blocks, CMP_BLOCK, KV_W), axis=1))
        return jnp.concatenate(rows, axis=0).astype(jnp.bfloat16)

    kc = jnp.dot(means(k_refs), wpk_ref[...], preferred_element_type=jnp.float32)
    kc_ref[0] = _rope(kc, cc_ref[...], scl_ref[...], sch_ref[...], ROT_DIM_A // 2)
    vc_ref[0] = jnp.dot(means(v_refs), wpv_ref[...], preferred_element_type=jnp.float32)


def _pool_t(cache):
    n_phys = cache.shape[1]
    return jnp.transpose(cache[0], (0, 2, 3, 1)).reshape(n_phys, KV_W, PAGE_SIZE)


def _cmp_pool(page_table, k_pool_t, v_pool_t, wts):
    db, n_pages = page_table.shape
    n = PAGES_PER_STEP
    blocks = PAGE_SIZE // CMP_BLOCK
    n_cmp = n_pages * blocks
    cmp_pos = jnp.arange(n_cmp, dtype=jnp.int32) * CMP_BLOCK + (CMP_BLOCK - 1)
    tc = _rope_tables(cmp_pos, ROT_DIM_A, HEAD_DIM_A)
    tab = pl.BlockSpec((n * blocks, LANES), lambda b, c, pt: (c, 0))
    wspec = pl.BlockSpec((KV_W, KV_W), lambda b, c, pt: (0, 0))
    out_spec = pl.BlockSpec((1, n * blocks, LANES), lambda b, c, pt: (b, c, 0))
    page = (KV_W, PAGE_SIZE)
    return pl.pallas_call(
        functools.partial(_cmp_pool_kernel, n=n),
        grid_spec=pltpu.PrefetchScalarGridSpec(
            num_scalar_prefetch=1, grid=(db, n_pages // n),
            in_specs=_page_specs(page, n) + _page_specs(page, n) + [tab] * 3 + [wspec] * 2,
            out_specs=[out_spec, out_spec]),
        out_shape=[jax.ShapeDtypeStruct((db, n_cmp, LANES), jnp.float32)] * 2,
        compiler_params=pltpu.CompilerParams(dimension_semantics=("arbitrary", "arbitrary")),
        name="cmp_pool",
    )(page_table, *([k_pool_t] * n), *([v_pool_t] * n), *tc, wts['wpk'], wts['wpv'])


def _new_token_scores(q, k_new, t_row, ds):
    cols, masks = [], []
    for s in range(ds):
        cols.append(jnp.sum(q * k_new[s:s + 1, :], axis=-1, keepdims=True))
        masks.append(t_row >= s)
    return cols, masks


def _merge_new_tokens(m, l, acc, cols, masks, v_new):
    m_new = m
    for c, k in zip(cols, masks):
        m_new = jnp.maximum(m_new, jnp.where(k, c, NEG))
    alpha = jnp.exp(m - m_new)
    l, acc = alpha * l, alpha * acc
    for s, (c, k) in enumerate(zip(cols, masks)):
        p = jnp.where(k, jnp.exp(c - m_new), 0.0)
        l = l + p
        acc = acc + p.astype(jnp.bfloat16).astype(jnp.float32) * v_new[s:s + 1, :]
    return m_new, l, acc


def _nsa_sample_kernel(pt_ref, *refs, n, n_pages):
    k_refs, v_refs = refs[:n], refs[n:2 * n]
    (qa_ref, new_ref, kc_ref, vc_ref, wk_ref, wv_ref, g_ref, sza_ref, out_ref,
     m_ref, l_ref, acc_ref, ocmp_ref, owin_ref, sel_ref) = refs[2 * n:]
    c = pl.program_id(1)
    ds = qa_ref.shape[1]
    rows = N_KV_A * GROUP_A * ds
    n_past_blocks = n_pages * PAGE_SIZE // SEL_BLOCK
    lane = lax.broadcasted_iota(jnp.int32, (ds, LANES), 1)
    low = lane < HEAD_DIM_A
    qa = qa_ref[0]
    q = jnp.concatenate([jnp.where(low if k == 0 else ~low, qa[:, g * LANES:(g + 1) * LANES], 0.0)
                         for k in range(N_KV_A) for g in range(GROUP_A)], axis=0)
    qb = q.astype(jnp.bfloat16)
    t_row = lax.broadcasted_iota(jnp.int32, (rows, 1), 0) % ds
    new = new_ref[0]

    @pl.when(c == 0)
    def _():
        kc = kc_ref[0].astype(jnp.bfloat16)
        vc = vc_ref[0].astype(jnp.bfloat16)
        s = lax.dot_general(qb, kc, _NT, preferred_element_type=jnp.float32)
        p = jnp.exp(s - jnp.max(s, axis=-1, keepdims=True))
        p = p / jnp.sum(p, axis=-1, keepdims=True)
        ocmp_ref[...] = jnp.dot(p.astype(jnp.bfloat16), vc, preferred_element_type=jnp.float32)
        gd = GROUP_A * ds
        imp = jnp.concatenate([sum(p[k * gd + g * ds:k * gd + (g + 1) * ds] for g in range(GROUP_A))
                               for k in range(N_KV_A)], axis=0)
        score = imp[:, :n_past_blocks] + imp[:, n_past_blocks:]
        blk = lax.broadcasted_iota(jnp.int32, score.shape, 1)
        forced = (blk == 0) | (blk == n_past_blocks - 1)
        sc = jnp.where(forced, jnp.inf, score)

        def count(r, cnt):
            other = pltpu.roll(sc, r, 1)
            return cnt + jnp.where(blk >= r, (other >= sc).astype(jnp.int32), (other > sc).astype(jnp.int32))

        cnt = lax.fori_loop(1, n_past_blocks, count, jnp.zeros(sc.shape, jnp.int32))
        sel = (cnt < N_SEL - 1).astype(jnp.bfloat16)
        sel_ref[...] = jnp.concatenate([sel[k * ds:(k + 1) * ds] for k in range(N_KV_A) for _ in range(GROUP_A)], axis=0)

        kw = wk_ref[0].astype(jnp.bfloat16)
        vw = wv_ref[0].astype(jnp.bfloat16)
        s = jnp.dot(qb, kw, preferred_element_type=jnp.float32)
        wb = s.shape[1]
        idx = lax.broadcasted_iota(jnp.int32, (rows, wb), 1)
        mask = idx > t_row + (wb - WINDOW)
        m = jnp.max(jnp.where(mask, s, NEG), axis=-1, keepdims=True)
        p = jnp.where(mask, jnp.exp(s - m), 0.0)
        l = jnp.sum(p, axis=-1, keepdims=True)
        acc = lax.dot_general(p.astype(jnp.bfloat16), vw, _NT, preferred_element_type=jnp.float32)
        cols, masks = _new_token_scores(q, new[:, 2 * KV_W:3 * KV_W], t_row, ds)
        m, l, acc = _merge_new_tokens(m, l, acc, cols, masks, new[:, 3 * KV_W:4 * KV_W])
        owin_ref[...] = acc / l

        cols, masks = _new_token_scores(q, new[:, 0:KV_W], t_row, ds)
        m0 = jnp.full((rows, 1), NEG, jnp.float32)
        m, l, acc = _merge_new_tokens(m0, jnp.zeros((rows, 1), jnp.float32), jnp.zeros((rows, LANES), jnp.float32),
                                      cols, masks, new[:, KV_W:2 * KV_W])
        m_ref[...], l_ref[...], acc_ref[...] = m, l, acc

    width = n * PAGE_SIZE
    per_page = PAGE_SIZE // SEL_BLOCK
    eblk = (lax.broadcasted_iota(jnp.int32, (n_past_blocks, width), 0)
            - lax.broadcasted_iota(jnp.int32, (n_past_blocks, width), 1) // SEL_BLOCK)
    expand = (eblk == c * (n * per_page)).astype(jnp.bfloat16)
    mask = jnp.dot(sel_ref[...], expand, preferred_element_type=jnp.float32) > 0.5
    s = jnp.concatenate([jnp.dot(qb, r[0].astype(jnp.bfloat16), preferred_element_type=jnp.float32) for r in k_refs], axis=1)
    m_old = m_ref[...]
    m_new = jnp.maximum(m_old, jnp.max(jnp.where(mask, s, NEG), axis=-1, keepdims=True))
    p = jnp.where(mask, jnp.exp(s - m_new), 0.0)
    alpha = jnp.exp(m_old - m_new)
    l_ref[...] = alpha * l_ref[...] + jnp.sum(p, axis=-1, keepdims=True)
    pb = p.astype(jnp.bfloat16)
    pv = sum(lax.dot_general(pb[:, i * PAGE_SIZE:(i + 1) * PAGE_SIZE], r[0].astype(jnp.bfloat16), _NT,
                             preferred_element_type=jnp.float32) for i, r in enumerate(v_refs))
    acc_ref[...] = alpha * acc_ref[...] + pv
    m_ref[...] = m_new

    @pl.when(c == pl.num_programs(1) - 1)
    def _():
        osel = acc_ref[...] / l_ref[...]
        gd = GROUP_A * ds
        for g in range(GROUP_A):
            pick = lambda o: jnp.where(low, o[g * ds:(g + 1) * ds], o[gd + g * ds:gd + (g + 1) * ds])
            cs = slice(g * LANES, (g + 1) * LANES)
            o = (g_ref[0, :, cs] * pick(ocmp_ref[...])
                 + g_ref[0, :, WIDTH_A + g * LANES:WIDTH_A + (g + 1) * LANES] * pick(osel)
                 + g_ref[0, :, 2 * WIDTH_A + g * LANES:2 * WIDTH_A + (g + 1) * LANES] * pick(owin_ref[...]))
            out_ref[0, :, cs] = o * sza_ref[0, :, cs]


def _nsa_sample(page_table, ps, kc, vc, k_pool_t, v_pool_t, win_k_t, win_v_t, db, ds):
    n = PAGES_PER_STEP
    n_pages = page_table.shape[1]
    n_cmp = kc.shape[1]
    wb = win_k_t.shape[2]
    f32 = jnp.float32
    per_b = lambda r, w: pl.BlockSpec((1, r, w), lambda b, c, pt: (b, 0, 0))
    rows = N_KV_A * GROUP_A * ds
    args = (ps['qa'].astype(f32).reshape(db, ds, WIDTH_A), ps['kvb'].astype(f32).reshape(db, ds, 4 * KV_W),
            kc, vc, win_k_t, win_v_t, ps['gates'].reshape(db, ds, 3 * WIDTH_A), ps['sza'].reshape(db, ds, WIDTH_A))
    in_specs = (_page_specs((KV_W, PAGE_SIZE), n) + _page_specs((KV_W, PAGE_SIZE), n)
                + [per_b(ds, WIDTH_A), per_b(ds, 4 * KV_W), per_b(n_cmp, LANES), per_b(n_cmp, LANES),
                   per_b(KV_W, wb), per_b(KV_W, wb), per_b(ds, 3 * WIDTH_A), per_b(ds, WIDTH_A)])
    return pl.pallas_call(
        functools.partial(_nsa_sample_kernel, n=n, n_pages=n_pages),
        grid_spec=pltpu.PrefetchScalarGridSpec(
            num_scalar_prefetch=1, grid=(db, n_pages // n), in_specs=in_specs,
            out_specs=per_b(ds, WIDTH_A),
            scratch_shapes=[pltpu.VMEM((rows, 1), f32), pltpu.VMEM((rows, 1), f32), pltpu.VMEM((rows, LANES), f32),
                            pltpu.VMEM((rows, LANES), f32), pltpu.VMEM((rows, LANES), f32),
                            pltpu.VMEM((rows, n_pages * PAGE_SIZE // SEL_BLOCK), jnp.bfloat16)]),
        out_shape=jax.ShapeDtypeStruct((db, ds, WIDTH_A), f32),
        compiler_params=pltpu.CompilerParams(dimension_semantics=("arbitrary", "arbitrary")),
        name="nsa_sample",
    )(page_table, *([k_pool_t] * n), *([v_pool_t] * n), *args)


def _mla_sample_kernel(pt_ref, *refs, n):
    ckv_refs, kpe_refs = refs[:n], refs[n:2 * n]
    q_ref, new_ref, szb_ref, wuv_ref, out_ref, m_ref, l_ref, acc_ref = refs[2 * n:]
    c = pl.program_id(1)
    ds = q_ref.shape[1]
    rows = N_HEADS_B * ds
    qw = 2 * LANES
    q = jnp.concatenate([q_ref[0, :, h * qw:(h + 1) * qw] for h in range(N_HEADS_B)], axis=0)
    q_abs = q[:, :KV_RANK].astype(jnp.bfloat16)
    q_pe = q[:, KV_RANK:].astype(jnp.bfloat16)
    t_row = lax.broadcasted_iota(jnp.int32, (rows, 1), 0) % ds

    @pl.when(c == 0)
    def _():
        new = new_ref[0]
        cols, masks = _new_token_scores(q, new, t_row, ds)
        cols = [x * MLA_SCALE for x in cols]
        m0 = jnp.full((rows, 1), NEG, jnp.float32)
        m, l, acc = _merge_new_tokens(m0, jnp.zeros((rows, 1), jnp.float32), jnp.zeros((rows, KV_RANK), jnp.float32),
                                      cols, masks, new[:, :KV_RANK])
        m_ref[...], l_ref[...], acc_ref[...] = m, l, acc

    pad = jnp.zeros((LANES - ROPE_DIM_B, PAGE_SIZE), jnp.float32)
    ckvs = [r[0].astype(jnp.bfloat16) for r in ckv_refs]
    s = jnp.concatenate(
        [lax.dot_general(q_abs, kv, _NT, preferred_element_type=jnp.float32)
         + jnp.dot(q_pe, jnp.concatenate([r[0], pad], axis=0).astype(jnp.bfloat16), preferred_element_type=jnp.float32)
         for kv, r in zip(ckvs, kpe_refs)], axis=1) * MLA_SCALE
    m_old = m_ref[...]
    m_new = jnp.maximum(m_old, jnp.max(s, axis=-1, keepdims=True))
    p = jnp.exp(s - m_new)
    alpha = jnp.exp(m_old - m_new)
    l_ref[...] = alpha * l_ref[...] + jnp.sum(p, axis=-1, keepdims=True)
    pb = p.astype(jnp.bfloat16)
    pv = sum(jnp.dot(pb[:, i * PAGE_SIZE:(i + 1) * PAGE_SIZE], kv, preferred_element_type=jnp.float32)
             for i, kv in enumerate(ckvs))
    acc_ref[...] = alpha * acc_ref[...] + pv
    m_ref[...] = m_new

    @pl.when(c == pl.num_programs(1) - 1)
    def _():
        o = acc_ref[...] / l_ref[...]
        wide = jnp.concatenate([o[h * ds:(h + 1) * ds] for h in range(N_HEADS_B)], axis=1)
        wide = jnp.concatenate([wide, jnp.zeros_like(wide)], axis=0).astype(jnp.bfloat16)
        ob = jnp.dot(wide, wuv_ref[...], preferred_element_type=jnp.float32)[:ds]
        out_ref[0] = ob * szb_ref[0]


def _mla_sample(page_table, ps, ckv_pool, kpe_pool_t, wts, db, ds):
    n = PAGES_PER_STEP
    n_pages = page_table.shape[1]
    f32 = jnp.float32
    per_b = lambda r, w: pl.BlockSpec((1, r, w), lambda b, c, pt: (b, 0, 0))
    rows = N_HEADS_B * ds
    args = (ps['qmla'].astype(f32).reshape(db, ds, N_HEADS_B * 2 * LANES), ps['kmla'].astype(f32).reshape(db, ds, 2 * KV_RANK),
            ps['szb'].reshape(db, ds, WIDTH_B), wts['wuv'])
    in_specs = (_page_specs((PAGE_SIZE, KV_RANK), n) + _page_specs((ROPE_DIM_B, PAGE_SIZE), n)
                + [per_b(ds, N_HEADS_B * 2 * LANES), per_b(ds, 2 * KV_RANK), per_b(ds, WIDTH_B),
                   pl.BlockSpec(wts['wuv'].shape, lambda b, c, pt: (0, 0))])
    return pl.pallas_call(
        functools.partial(_mla_sample_kernel, n=n),
        grid_spec=pltpu.PrefetchScalarGridSpec(
            num_scalar_prefetch=1, grid=(db, n_pages // n), in_specs=in_specs,
            out_specs=per_b(ds, WIDTH_B),
            scratch_shapes=[pltpu.VMEM((rows, 1), f32), pltpu.VMEM((rows, 1), f32), pltpu.VMEM((rows, KV_RANK), f32)]),
        out_shape=jax.ShapeDtypeStruct((db, ds, WIDTH_B), f32),
        compiler_params=pltpu.CompilerParams(dimension_semantics=("arbitrary", "arbitrary")),
        name="mla_sample",
    )(page_table, *([ckv_pool] * n), *([kpe_pool_t] * n), *args)


def _kv_heads(x, lead):
    return x.reshape(*lead, N_KV_A, HEAD_DIM_A)


def kernel(x_prompt, x_sample, cache_k_cmp, cache_v_cmp, cache_k_sel, cache_v_sel, cache_ckv, cache_kpe, state_win_k, state_win_v, page_table, ln_in_g, ln_in_b, w_in, w_phi_k, w_phi_v, q_norm_g, w_uq, kv_norm_g, w_uk, w_uv, w_o, ln_g, ln_b):
    depth = w_in.shape[0]
    assert depth == 1, "one layer only"
    alpha = (2.0 * depth) ** 0.25
    b, seq, _ = x_prompt.shape
    db, ds, _ = x_sample.shape
    past = page_table.shape[1] * PAGE_SIZE
    l = 0
    wts = _prep_weights(w_in[l], w_phi_k[l], w_phi_v[l], w_uq[l], w_uk[l], w_uv[l], w_o[l])

    xp = x_prompt.reshape(b * seq, D_MODEL)
    pos_p = jnp.arange(seq, dtype=jnp.int32)
    cmp_pos = jnp.arange(seq // CMP_BLOCK, dtype=jnp.int32) * CMP_BLOCK + (CMP_BLOCK - 1)
    pp = _project(xp, pos_p, cmp_pos, wts, ln_in_g, ln_in_b, q_norm_g[l], kv_norm_g[l], tm=256, with_cmp=True)
    mixed_a = _nsa_prompt(pp, b, seq)
    mixed_b = _mla_prompt(pp, wts, b, seq)
    y_p = _out_project(xp, mixed_a, mixed_b, wts, ln_in_g, ln_in_b, ln_g[l], ln_b[l], tm=256, alpha=alpha)

    kvp = pp['kv'].reshape(b, seq, 6, N_KV_A, HEAD_DIM_A)
    wl = min(WINDOW, seq)
    p_out = [kvp[None, :, :, i] for i in range(4)]
    p_ckv = pp['ckv'].reshape(1, b, seq, KV_RANK)
    p_kpe = pp['kpe'].reshape(b, seq, LANES)[None, :, :, :ROPE_DIM_B]
    p_wk, p_wv = kvp[None, :, seq - wl:, 4], kvp[None, :, seq - wl:, 5]

    xs = x_sample.reshape(db * ds, D_MODEL)
    tm_s = 256
    pos_s = jnp.tile(past + jnp.arange(ds, dtype=jnp.int32), tm_s // ds)
    ps = _project(xs, pos_s, jnp.zeros((tm_s // CMP_BLOCK,), jnp.int32), wts, ln_in_g, ln_in_b, q_norm_g[l], kv_norm_g[l],
                  tm=tm_s, with_cmp=False)
    kc_s, vc_s = _cmp_pool(page_table, _pool_t(cache_k_cmp), _pool_t(cache_v_cmp), wts)
    even_odd = lambda x: jnp.concatenate([x[:, 0::2], x[:, 1::2]], axis=1)
    win_t = lambda st: jnp.transpose(st[l], (0, 2, 3, 1)).reshape(db, KV_W, st.shape[2])
    mixed_a_s = _nsa_sample(page_table, ps, even_odd(kc_s), even_odd(vc_s), _pool_t(cache_k_sel), _pool_t(cache_v_sel),
                            win_t(state_win_k), win_t(state_win_v), db, ds)
    n_phys = cache_ckv.shape[1]
    kpe_pool_t = jnp.transpose(cache_kpe[l], (0, 2, 1))
    mixed_b_s = _mla_sample(page_table, ps, cache_ckv[l], kpe_pool_t, wts, db, ds)
    y_s = _out_project(xs, mixed_a_s.reshape(db * ds, WIDTH_A).astype(jnp.bfloat16),
                       mixed_b_s.reshape(db * ds, WIDTH_B).astype(jnp.bfloat16),
                       wts, ln_in_g, ln_in_b, ln_g[l], ln_b[l], tm=tm_s, alpha=alpha)

    kvs = ps['kv'].reshape(db, ds, 6, N_KV_A, HEAD_DIM_A)
    s_out = [kvs[None, :, :, i] for i in range(4)]
    s_ckv = ps['ckv'].reshape(1, db, ds, KV_RANK)
    s_kpe = ps['kpe'].reshape(db, ds, LANES)[None, :, :, :ROPE_DIM_B]
    wb = state_win_k.shape[2]
    s_wk = jnp.concatenate([state_win_k[l], kvs[:, :, 4]], axis=1)[None, :, -wb:]
    s_wv = jnp.concatenate([state_win_v[l], kvs[:, :, 5]], axis=1)[None, :, -wb:]
    return (y_p.reshape(b, seq, D_MODEL), y_s.reshape(db, ds, D_MODEL), *p_out, p_ckv, p_kpe, p_wk, p_wv,
            *s_out, s_ckv, s_kpe, s_wk, s_wv)
```

```python
import functools

import numpy as np
import jax
import jax.numpy as jnp
from jax import lax
from jax.experimental import pallas as pl
from jax.experimental.pallas import tpu as pltpu

D_MODEL = 1024
HEAD_DIM_A = 64
N_HEADS_A = 8
N_KV_A = 2
GROUP_A = 4
ROT_DIM_A = 16
CMP_BLOCK = 32
SEL_BLOCK = 64
N_SEL = 16
WINDOW = 512
N_HEADS_B = 8
NOPE_DIM = 64
ROPE_DIM_B = 32
V_DIM = 64
Q_RANK = 256
KV_RANK = 128
ROPE_THETA = 500000.0
LN_EPS = 1e-5
RMS_EPS = 1e-6
PAGE_SIZE = 128
WIDTH_A = 512
WIDTH_B = 512
KV_W = 128
SCALE_A = HEAD_DIM_A ** -0.5
MLA_SCALE = (NOPE_DIM + ROPE_DIM_B) ** -0.5
IN_WIDTH = 2744

LANES = 128
NEG = -1e30

OFF_QA, OFF_KV, OFF_GATE, OFF_ZA, OFF_CQ, OFF_CKV, OFF_KPE, OFF_ZB = 0, 512, 1280, 1304, 1816, 2072, 2200, 2232

SEG_QA, SEG_KV, SEG_G, SEG_ZA, SEG_CQ, SEG_CKV, SEG_KPE, SEG_ZB = 0, 512, 1280, 2816, 3328, 3584, 3712, 3840
EXT_WIDTH = 4352

_NT = (((1,), (1,)), ((), ()))


def _paired_head(lane):
    g, s = lane // LANES, (lane % LANES) // HEAD_DIM_A
    return g + GROUP_A * s, lane % HEAD_DIM_A


def _proj_row_index():
    idx = np.full((EXT_WIDTH,), IN_WIDTH, np.int32)
    for lane in range(WIDTH_A):
        h, d = _paired_head(lane)
        idx[SEG_QA + lane] = OFF_QA + h * HEAD_DIM_A + d
        idx[SEG_ZA + lane] = OFF_ZA + h * HEAD_DIM_A + d
        for j in range(3):
            idx[SEG_G + j * WIDTH_A + lane] = OFF_GATE + h * 3 + j
    idx[SEG_KV:SEG_KV + 6 * KV_W] = OFF_KV + np.arange(6 * KV_W)
    idx[SEG_CQ:SEG_CQ + Q_RANK] = OFF_CQ + np.arange(Q_RANK)
    idx[SEG_CKV:SEG_CKV + KV_RANK] = OFF_CKV + np.arange(KV_RANK)
    idx[SEG_KPE:SEG_KPE + ROPE_DIM_B] = OFF_KPE + np.arange(ROPE_DIM_B)
    idx[SEG_ZB:SEG_ZB + WIDTH_B] = OFF_ZB + np.arange(WIDTH_B)
    return idx


def _rope_tables(pos, rot_dim, period):
    half = rot_dim // 2
    inv = ROPE_THETA ** (-jnp.arange(half, dtype=jnp.float32) * 2.0 / rot_dim)
    ang = pos.astype(jnp.float32)[:, None] * inv
    cos, sin = jnp.cos(ang), jnp.sin(ang)
    n = pos.shape[0]
    pad = jnp.zeros((n, period - rot_dim), jnp.float32)
    c = jnp.concatenate([cos, cos, pad + 1.0], axis=1)
    s_lo = jnp.concatenate([-sin, jnp.zeros_like(sin), pad], axis=1)
    s_hi = jnp.concatenate([jnp.zeros_like(sin), sin, pad], axis=1)
    rep = LANES // period
    return tuple(jnp.tile(t, (1, rep)) for t in (c, s_lo, s_hi))


def _rope(x, c, s_lo, s_hi, half):
    w = x.shape[1]
    return x * c + pltpu.roll(x, w - half, 1) * s_lo + pltpu.roll(x, half, 1) * s_hi


def _tile_lanes(t, width):
    return t if width == LANES else jnp.concatenate([t] * (width // LANES), axis=1)


def _layernorm(x, g, b):
    mu = jnp.mean(x, axis=-1, keepdims=True)
    xc = x - mu
    var = jnp.mean(xc * xc, axis=-1, keepdims=True)
    return xc * lax.rsqrt(var + LN_EPS) * g + b


def _rmsnorm(x, g):
    return x * lax.rsqrt(jnp.mean(x * x, axis=-1, keepdims=True) + RMS_EPS) * g


def _sigmoid(x):
    return 1.0 / (1.0 + jnp.exp(-x))


def _proj_kernel(x_ref, lng_ref, lnb_ref, w_ref, qng_ref, wuq_ref, wuk_ref, kvng_ref,
                 ca_ref, sal_ref, sah_ref, cb_ref, sbl_ref, sbh_ref,
                 cc_ref, scl_ref, sch_ref, wpk_ref, wpv_ref,
                 qa_ref, kv_ref, kvb_ref, g_ref, sza_ref, szb_ref, ckv_ref, kpe_ref, kmla_ref, qmla_ref,
                 kc_ref, vc_ref, *, with_cmp):
    hn = _layernorm(x_ref[...], lng_ref[...], lnb_ref[...]).astype(jnp.bfloat16)

    def seg(lo, width):
        return lax.dot_general(hn, w_ref[lo:lo + width, :], _NT, preferred_element_type=jnp.float32)

    ca, sal, sah = ca_ref[...], sal_ref[...], sah_ref[...]
    half_a = ROT_DIM_A // 2

    qa = seg(SEG_QA, WIDTH_A)
    qa = _rope(qa, _tile_lanes(ca, WIDTH_A), _tile_lanes(sal, WIDTH_A), _tile_lanes(sah, WIDTH_A), half_a)
    qa_ref[...] = (qa * SCALE_A).astype(jnp.bfloat16)

    kv = seg(SEG_KV, 6 * KV_W)
    k_cmp, v_cmp = kv[:, 0:KV_W], kv[:, KV_W:2 * KV_W]
    k_sel = _rope(kv[:, 2 * KV_W:3 * KV_W], ca, sal, sah, half_a)
    v_sel = kv[:, 3 * KV_W:4 * KV_W]
    k_win = _rope(kv[:, 4 * KV_W:5 * KV_W], ca, sal, sah, half_a)
    v_win = kv[:, 5 * KV_W:6 * KV_W]
    kv_ref[:, 0:2 * KV_W] = kv[:, 0:2 * KV_W]
    kv_ref[:, 2 * KV_W:3 * KV_W] = k_sel
    kv_ref[:, 3 * KV_W:4 * KV_W] = v_sel
    kv_ref[:, 4 * KV_W:5 * KV_W] = k_win
    kv_ref[:, 5 * KV_W:6 * KV_W] = v_win
    kvb_ref[:, 0:KV_W] = k_sel.astype(jnp.bfloat16)
    kvb_ref[:, KV_W:2 * KV_W] = v_sel.astype(jnp.bfloat16)
    kvb_ref[:, 2 * KV_W:3 * KV_W] = k_win.astype(jnp.bfloat16)
    kvb_ref[:, 3 * KV_W:4 * KV_W] = v_win.astype(jnp.bfloat16)

    for j in range(3):
        g_ref[:, j * WIDTH_A:(j + 1) * WIDTH_A] = _sigmoid(seg(SEG_G + j * WIDTH_A, WIDTH_A))

    za = seg(SEG_ZA, WIDTH_A)
    sza_ref[...] = za * _sigmoid(za)
    zb = seg(SEG_ZB, WIDTH_B)
    szb_ref[...] = zb * _sigmoid(zb)

    ckv = _rmsnorm(seg(SEG_CKV, KV_RANK), kvng_ref[...])
    ckv_ref[...] = ckv
    half_b = ROPE_DIM_B // 2
    cb, sbl, sbh = cb_ref[...], sbl_ref[...], sbh_ref[...]
    kpe = _rope(seg(SEG_KPE, LANES), cb, sbl, sbh, half_b)
    kpe_ref[...] = kpe
    kmla_ref[:, 0:KV_RANK] = ckv.astype(jnp.bfloat16)
    kmla_ref[:, KV_RANK:2 * KV_RANK] = kpe.astype(jnp.bfloat16)

    cq = _rmsnorm(seg(SEG_CQ, Q_RANK), qng_ref[...]).astype(jnp.bfloat16)
    qb = jnp.dot(cq, wuq_ref[...], preferred_element_type=jnp.float32)
    nope_w = N_HEADS_B * NOPE_DIM
    q_abs = jnp.dot(qb[:, :nope_w].astype(jnp.bfloat16), wuk_ref[...], preferred_element_type=jnp.float32)
    pe_w = N_HEADS_B * LANES
    q_pe = _rope(qb[:, nope_w:], _tile_lanes(cb, pe_w), _tile_lanes(sbl, pe_w), _tile_lanes(sbh, pe_w), half_b)
    for h in range(N_HEADS_B):
        qmla_ref[:, 2 * h * LANES:(2 * h + 1) * LANES] = q_abs[:, h * LANES:(h + 1) * LANES].astype(jnp.bfloat16)
        qmla_ref[:, (2 * h + 1) * LANES:(2 * h + 2) * LANES] = q_pe[:, h * LANES:(h + 1) * LANES].astype(jnp.bfloat16)

    if with_cmp:
        tm = k_cmp.shape[0]
        nblk = tm // CMP_BLOCK
        k_mean = jnp.mean(k_cmp.reshape(nblk, CMP_BLOCK, KV_W), axis=1).astype(jnp.bfloat16)
        v_mean = jnp.mean(v_cmp.reshape(nblk, CMP_BLOCK, KV_W), axis=1).astype(jnp.bfloat16)
        kc = jnp.dot(k_mean, wpk_ref[...], preferred_element_type=jnp.float32)
        kc_ref[...] = _rope(kc, cc_ref[...], scl_ref[...], sch_ref[...], half_a)
        vc_ref[...] = jnp.dot(v_mean, wpv_ref[...], preferred_element_type=jnp.float32)
    else:
        kc_ref[...] = jnp.zeros_like(kc_ref)
        vc_ref[...] = jnp.zeros_like(vc_ref)


def _block_diag2(w):
    z = jnp.zeros((HEAD_DIM_A, HEAD_DIM_A), w.dtype)
    return jnp.concatenate([jnp.concatenate([w[0], z], 1), jnp.concatenate([z, w[1]], 1)], 0)


def _prep_weights(w_in, w_phi_k, w_phi_v, w_uq, w_uk, w_uv, w_o):
    wt = jnp.swapaxes(w_in, 0, 1)
    wt = jnp.concatenate([wt, jnp.zeros((1, D_MODEL), wt.dtype)], axis=0)
    w_ext = jnp.take(wt, jnp.asarray(_proj_row_index()), axis=0).astype(jnp.bfloat16)
    nope = w_uq[:, :, :NOPE_DIM].reshape(Q_RANK, N_HEADS_B * NOPE_DIM)
    pe = jnp.pad(w_uq[:, :, NOPE_DIM:], ((0, 0), (0, 0), (0, LANES - ROPE_DIM_B))).reshape(Q_RANK, N_HEADS_B * LANES)
    wuq = jnp.concatenate([nope, pe], axis=1).astype(jnp.bfloat16)
    eye = jnp.eye(N_HEADS_B, dtype=w_uk.dtype)
    wuk = jnp.einsum('rhd,hg->hdgr', w_uk, eye).reshape(N_HEADS_B * NOPE_DIM, N_HEADS_B * KV_RANK).astype(jnp.bfloat16)
    wuv = jnp.einsum('rhd,hg->hrgd', w_uv, eye).reshape(N_HEADS_B * KV_RANK, N_HEADS_B * V_DIM).astype(jnp.bfloat16)
    heads, dims = _paired_head(np.arange(WIDTH_A))
    woa = jnp.take(w_o[:WIDTH_A], jnp.asarray(heads * HEAD_DIM_A + dims), axis=0).astype(jnp.bfloat16)
    wob = w_o[WIDTH_A:].astype(jnp.bfloat16)
    wpk = _block_diag2(w_phi_k).astype(jnp.bfloat16)
    wpv = _block_diag2(w_phi_v).astype(jnp.bfloat16)
    return dict(w_ext=w_ext, wuq=wuq, wuk=wuk, wuv=wuv, woa=woa, wob=wob, wpk=wpk, wpv=wpv)


def _project(x, pos_rows, cmp_pos, wts, ln_in_g, ln_in_b, q_norm_g, kv_norm_g, *, tm, with_cmp):
    m = x.shape[0]
    n_tiles = m // tm
    ta = _rope_tables(pos_rows, ROT_DIM_A, HEAD_DIM_A)
    tb = _rope_tables(pos_rows, ROPE_DIM_B, LANES)
    tc = _rope_tables(cmp_pos, ROT_DIM_A, HEAD_DIM_A)
    tab_tiles = pos_rows.shape[0] // tm
    nblk = tm // CMP_BLOCK
    cmp_tiles = max(cmp_pos.shape[0] // nblk, 1)

    row = lambda w: pl.BlockSpec((tm, w), lambda i: (i, 0))
    full = lambda a: pl.BlockSpec(a.shape, lambda i: (0,) * a.ndim)
    tab = pl.BlockSpec((tm, LANES), lambda i: (i % tab_tiles, 0))
    ctab = pl.BlockSpec((nblk, LANES), lambda i: (i % cmp_tiles, 0))
    vec = lambda v: v.reshape(1, -1)
    args = (x, vec(ln_in_g), vec(ln_in_b), wts['w_ext'], vec(q_norm_g), wts['wuq'], wts['wuk'], vec(kv_norm_g),
            *ta, *tb, *tc, wts['wpk'], wts['wpv'])
    in_specs = [row(D_MODEL)] + [full(a) for a in args[1:8]] + [tab] * 6 + [ctab] * 3 + [full(wts['wpk']), full(wts['wpv'])]
    f32, bf16 = jnp.float32, jnp.bfloat16
    outs = [(WIDTH_A, bf16), (6 * KV_W, f32), (4 * KV_W, bf16), (3 * WIDTH_A, f32), (WIDTH_A, f32), (WIDTH_B, f32),
            (KV_RANK, f32), (LANES, f32), (2 * KV_RANK, bf16), (N_HEADS_B * 2 * LANES, bf16)]
    out_shape = [jax.ShapeDtypeStruct((m, w), dt) for w, dt in outs]
    out_specs = [row(w) for w, _ in outs]
    out_shape += [jax.ShapeDtypeStruct((n_tiles * nblk, LANES), f32)] * 2
    out_specs += [pl.BlockSpec((nblk, LANES), lambda i: (i, 0))] * 2
    names = ('qa', 'kv', 'kvb', 'gates', 'sza', 'szb', 'ckv', 'kpe', 'kmla', 'qmla', 'kc', 'vc')
    res = pl.pallas_call(
        functools.partial(_proj_kernel, with_cmp=with_cmp),
        grid=(n_tiles,), in_specs=in_specs, out_specs=out_specs, out_shape=out_shape,
        compiler_params=pltpu.CompilerParams(dimension_semantics=("arbitrary",), vmem_limit_bytes=56 * 2**20),
        name="in_proj",
    )(*args)
    return dict(zip(names, res))


def _out_kernel(x_ref, lig_ref, lib_ref, ma_ref, mb_ref, woa_ref, wob_ref, g_ref, b_ref, y_ref, *, alpha):
    h = _layernorm(x_ref[...], lig_ref[...], lib_ref[...])
    o = jnp.dot(ma_ref[...], woa_ref[...], preferred_element_type=jnp.float32)
    o = o + jnp.dot(mb_ref[...], wob_ref[...], preferred_element_type=jnp.float32)
    y_ref[...] = _layernorm(alpha * h + o, g_ref[...], b_ref[...])


def _out_project(x, mixed_a, mixed_b, wts, ln_in_g, ln_in_b, ln_g, ln_b, *, tm, alpha):
    m = x.shape[0]
    row = lambda w: pl.BlockSpec((tm, w), lambda i: (i, 0))
    full = lambda a: pl.BlockSpec(a.shape, lambda i: (0,) * a.ndim)
    vec = lambda v: v.reshape(1, -1)
    args = (x, vec(ln_in_g), vec(ln_in_b), mixed_a, mixed_b, wts['woa'], wts['wob'], vec(ln_g), vec(ln_b))
    in_specs = [row(D_MODEL), full(args[1]), full(args[2]), row(WIDTH_A), row(WIDTH_B),
                full(args[5]), full(args[6]), full(args[7]), full(args[8])]
    return pl.pallas_call(
        functools.partial(_out_kernel, alpha=alpha),
        grid=(m // tm,), in_specs=in_specs, out_specs=row(D_MODEL),
        out_shape=jax.ShapeDtypeStruct((m, D_MODEL), jnp.float32),
        compiler_params=pltpu.CompilerParams(dimension_semantics=("arbitrary",)),
        name="out_proj",
    )(*args)


def _softmax_update(s, mask, m_ref, l_ref, acc_ref, v, nt_v=False):
    m_old = m_ref[...]
    m_new = jnp.maximum(m_old, jnp.max(jnp.where(mask, s, NEG), axis=-1, keepdims=True))
    p = jnp.where(mask, jnp.exp(s - m_new), 0.0)
    alpha = jnp.exp(m_old - m_new)
    l_ref[...] = alpha * l_ref[...] + jnp.sum(p, axis=-1, keepdims=True)
    pb = p.astype(jnp.bfloat16)
    if nt_v:
        pv = lax.dot_general(pb, v, _NT, preferred_element_type=jnp.float32)
    else:
        pv = jnp.dot(pb, v, preferred_element_type=jnp.float32)
    acc_ref[...] = alpha * acc_ref[...] + pv
    m_ref[...] = m_new


def _softmax_init(m_ref, l_ref, acc_ref):
    m_ref[...] = jnp.full_like(m_ref, NEG)
    l_ref[...] = jnp.zeros_like(l_ref)
    acc_ref[...] = jnp.zeros_like(acc_ref)


def _softmax_finish(l_ref, acc_ref):
    l = l_ref[...]
    return acc_ref[...] / jnp.where(l > 0.0, l, 1.0)


def _top_blocks(score, lane, cur, n_valid):
    forced = (lane == 0) | (lane == cur) | (lane == cur - 1)
    sc = jnp.where(lane <= cur, jnp.where(forced, jnp.inf, score), -jnp.inf)
    sc = jnp.where(lane < n_valid, sc, -jnp.inf)
    w = sc.shape[1]
    cnt = jnp.zeros(sc.shape, jnp.int32)
    for r in range(1, n_valid):
        cnt += (pltpu.roll(sc, r, 1) >= sc).astype(jnp.int32)
        cnt += (pltpu.roll(sc, w - r, 1) > sc).astype(jnp.int32)
    return (cnt < N_SEL) & (lane <= cur) & (lane < n_valid)


def _nsa_prompt_kernel(qa_ref, kvb_ref, kc_ref, vc_ref, g_ref, sza_ref, out_ref,
                       m_ref, l_ref, acc_ref, *, tq, tk, seq):
    i = pl.program_id(1)
    q0 = i * tq
    rows = GROUP_A * tq
    lane = lax.broadcasted_iota(jnp.int32, (tq, LANES), 1)
    t_q = q0 + lax.broadcasted_iota(jnp.int32, (tq, LANES), 0)
    low = lane < HEAD_DIM_A
    n_sel_blocks = seq // SEL_BLOCK

    qa = qa_ref[...]
    kc = kc_ref[0].astype(jnp.bfloat16)
    vc = vc_ref[0].astype(jnp.bfloat16)
    n_half = seq // CMP_BLOCK // 2
    blk = jnp.where(lane < n_half, 2 * lane, 2 * (lane - n_half) + 1)
    cmp_mask1 = (blk * CMP_BLOCK + CMP_BLOCK - 1 <= t_q) & (lane < 2 * n_half)
    cmp_mask = jnp.concatenate([cmp_mask1] * GROUP_A, axis=0)

    kcol = lax.broadcasted_iota(jnp.int32, (tq, tk), 1)
    trow = q0 + lax.broadcasted_iota(jnp.int32, (tq, tk), 0)
    eblk = lax.broadcasted_iota(jnp.int32, (LANES, tk), 0) - lax.broadcasted_iota(jnp.int32, (LANES, tk), 1) // SEL_BLOCK

    o_cmp, o_sel, o_win = [], [], []
    for k in range(N_KV_A):
        keep = low if k == 0 else ~low
        qk = jnp.concatenate([jnp.where(keep, qa[:, g * LANES:(g + 1) * LANES], 0) for g in range(GROUP_A)], axis=0)

        s = lax.dot_general(qk, kc, _NT, preferred_element_type=jnp.float32)
        m = jnp.max(jnp.where(cmp_mask, s, NEG), axis=-1, keepdims=True)
        p = jnp.where(cmp_mask, jnp.exp(s - m), 0.0)
        l = jnp.sum(p, axis=-1, keepdims=True)
        p = p / jnp.where(l > 0.0, l, 1.0)
        o_cmp.append(jnp.dot(p.astype(jnp.bfloat16), vc, preferred_element_type=jnp.float32))
        imp = p[0:tq] + p[tq:2 * tq] + p[2 * tq:3 * tq] + p[3 * tq:4 * tq]
        score = imp + pltpu.roll(imp, LANES - n_half, 1)
        cur = t_q // SEL_BLOCK
        sel = lax.cond(q0 + tq > N_SEL * SEL_BLOCK,
                       lambda: _top_blocks(score, lane, cur, n_sel_blocks).astype(jnp.float32),
                       lambda: ((lane <= cur) & (lane < n_sel_blocks)).astype(jnp.float32)).astype(jnp.bfloat16)

        _softmax_init(m_ref, l_ref, acc_ref)
        n_tiles = (q0 + tq + tk - 1) // tk

        def sel_step(j, carry):
            k0 = pl.multiple_of(j * tk, tk)
            kt = kvb_ref[pl.ds(k0, tk), 0:KV_W]
            vt = kvb_ref[pl.ds(k0, tk), KV_W:2 * KV_W]
            s = lax.dot_general(qk, kt, _NT, preferred_element_type=jnp.float32)
            expand = (eblk == j * (tk // SEL_BLOCK)).astype(jnp.bfloat16)
            picked = jnp.dot(sel, expand, preferred_element_type=jnp.float32) > 0.5
            mask1 = picked & (k0 + kcol <= trow)
            _softmax_update(s, jnp.concatenate([mask1] * GROUP_A, axis=0), m_ref, l_ref, acc_ref, vt)
            return carry

        lax.fori_loop(0, n_tiles, sel_step, 0)
        o_sel.append(_softmax_finish(l_ref, acc_ref))

        _softmax_init(m_ref, l_ref, acc_ref)
        j_lo = jnp.maximum(q0 - WINDOW + 1, 0) // tk

        def win_step(j, carry):
            k0 = pl.multiple_of(j * tk, tk)
            kt = kvb_ref[pl.ds(k0, tk), 2 * KV_W:3 * KV_W]
            vt = kvb_ref[pl.ds(k0, tk), 3 * KV_W:4 * KV_W]
            s = lax.dot_general(qk, kt, _NT, preferred_element_type=jnp.float32)
            kp = k0 + kcol
            mask1 = (kp <= trow) & (kp > trow - WINDOW)
            _softmax_update(s, jnp.concatenate([mask1] * GROUP_A, axis=0), m_ref, l_ref, acc_ref, vt)
            return carry

        lax.fori_loop(j_lo, n_tiles, win_step, 0)
        o_win.append(_softmax_finish(l_ref, acc_ref))

    for g in range(GROUP_A):
        r = slice(g * tq, (g + 1) * tq)
        c = slice(g * LANES, (g + 1) * LANES)
        pick = lambda o: jnp.where(low, o[0][r], o[1][r])
        o = (g_ref[:, c] * pick(o_cmp) + g_ref[:, WIDTH_A + g * LANES:WIDTH_A + (g + 1) * LANES] * pick(o_sel)
             + g_ref[:, 2 * WIDTH_A + g * LANES:2 * WIDTH_A + (g + 1) * LANES] * pick(o_win))
        out_ref[:, c] = (o * sza_ref[:, c]).astype(jnp.bfloat16)


def _permute_cmp_blocks(x, b, n):
    x = x.reshape(b, n // 2, 2, LANES)
    x = jnp.concatenate([x[:, :, 0], x[:, :, 1]], axis=1)
    return jnp.pad(x, ((0, 0), (0, LANES - n), (0, 0)))


def _nsa_prompt(pr, b, seq, *, tq=128, tk=512):
    n_cmp = seq // CMP_BLOCK
    kc = _permute_cmp_blocks(pr['kc'], b, n_cmp)
    vc = _permute_cmp_blocks(pr['vc'], b, n_cmp)
    nq = seq // tq
    row = lambda w: pl.BlockSpec((tq, w), lambda bi, i: (bi * nq + i, 0))
    per_b = lambda w: pl.BlockSpec((seq, w), lambda bi, i: (bi, 0))
    cmp_spec = pl.BlockSpec((1, LANES, LANES), lambda bi, i: (bi, 0, 0))
    rows = GROUP_A * tq
    return pl.pallas_call(
        functools.partial(_nsa_prompt_kernel, tq=tq, tk=tk, seq=seq),
        grid=(b, nq),
        in_specs=[row(WIDTH_A), per_b(4 * KV_W), cmp_spec, cmp_spec, row(3 * WIDTH_A), row(WIDTH_A)],
        out_specs=row(WIDTH_A),
        out_shape=jax.ShapeDtypeStruct((b * seq, WIDTH_A), jnp.bfloat16),
        scratch_shapes=[pltpu.VMEM((rows, 1), jnp.float32), pltpu.VMEM((rows, 1), jnp.float32),
                        pltpu.VMEM((rows, LANES), jnp.float32)],
        compiler_params=pltpu.CompilerParams(dimension_semantics=("arbitrary", "arbitrary")),
        name="nsa_prompt",
    )(pr['qa'], pr['kvb'], kc, vc, pr['gates'], pr['sza'])


def _mla_prompt_kernel(q_ref, k_ref, szb_ref, wuv_ref, out_ref, m_ref, l_ref, acc_ref, olat_ref, *, tq, tk, hg):
    i = pl.program_id(1)
    q0 = i * tq
    kcol = lax.broadcasted_iota(jnp.int32, (tq, tk), 1)
    trow = q0 + lax.broadcasted_iota(jnp.int32, (tq, tk), 0)
    n_tiles = (q0 + tq + tk - 1) // tk
    qw = 2 * LANES
    for grp in range(N_HEADS_B // hg):
        q = jnp.concatenate([q_ref[:, (grp * hg + h) * qw:(grp * hg + h + 1) * qw] for h in range(hg)], axis=0)
        _softmax_init(m_ref, l_ref, acc_ref)

        def step(j, carry):
            k0 = pl.multiple_of(j * tk, tk)
            kt = k_ref[pl.ds(k0, tk), :]
            s = lax.dot_general(q, kt, _NT, preferred_element_type=jnp.float32) * MLA_SCALE
            mask1 = k0 + kcol <= trow
            _softmax_update(s, jnp.concatenate([mask1] * hg, axis=0), m_ref, l_ref, acc_ref, kt[:, 0:KV_RANK])
            return carry

        lax.fori_loop(0, n_tiles, step, 0)
        o = _softmax_finish(l_ref, acc_ref)
        for h in range(hg):
            c = (grp * hg + h) * KV_RANK
            olat_ref[:, c:c + KV_RANK] = o[h * tq:(h + 1) * tq].astype(jnp.bfloat16)
    ob = jnp.dot(olat_ref[...], wuv_ref[...], preferred_element_type=jnp.float32)
    out_ref[...] = (ob * szb_ref[...]).astype(jnp.bfloat16)


def _mla_prompt(pr, wts, b, seq, *, tq=128, tk=512, hg=8):
    nq = seq // tq
    row = lambda w: pl.BlockSpec((tq, w), lambda bi, i: (bi * nq + i, 0))
    per_b = lambda w: pl.BlockSpec((seq, w), lambda bi, i: (bi, 0))
    rows = hg * tq
    return pl.pallas_call(
        functools.partial(_mla_prompt_kernel, tq=tq, tk=tk, hg=hg),
        grid=(b, nq),
        in_specs=[row(N_HEADS_B * 2 * LANES), per_b(2 * KV_RANK), row(WIDTH_B),
                  pl.BlockSpec(wts['wuv'].shape, lambda bi, i: (0, 0))],
        out_specs=row(WIDTH_B),
        out_shape=jax.ShapeDtypeStruct((b * seq, WIDTH_B), jnp.bfloat16),
        scratch_shapes=[pltpu.VMEM((rows, 1), jnp.float32), pltpu.VMEM((rows, 1), jnp.float32),
                        pltpu.VMEM((rows, KV_RANK), jnp.float32), pltpu.VMEM((tq, N_HEADS_B * KV_RANK), jnp.bfloat16)],
        compiler_params=pltpu.CompilerParams(dimension_semantics=("arbitrary", "arbitrary")),
        name="mla_prompt",
    )(pr['qmla'], pr['kmla'], pr['szb'], wts['wuv'])


PAGES_PER_CHUNK = 16


def _paged_pipeline(pt_ref, pools, bufs, sem, n, compute):
    b, c = pl.program_id(0), pl.program_id(1)
    nb, nc = pl.num_programs(0), pl.num_programs(1)

    def copies(bb, chunk, slot, start):
        for k in range(len(pools)):
            for i in range(n):
                page = pt_ref[bb, chunk * n + i] if start else 0
                cp = pltpu.make_async_copy(pools[k].at[page], bufs[k].at[slot, i], sem.at[k, slot])
                if start:
                    cp.start()
                else:
                    cp.wait()

    @pl.when((b == 0) & (c == 0))
    def _():
        copies(b, 0, 0, True)

    copies(b, 2 * c + 1, 1, True)
    copies(b, 0, 0, False)
    compute(0, 2 * c)

    last_c = c == nc - 1
    next_b = jnp.where(last_c, b + 1, b)
    next_c = jnp.where(last_c, 0, c + 1)

    @pl.when(jnp.logical_not(last_c & (b == nb - 1)))
    def _():
        copies(next_b, 2 * next_c, 0, True)

    copies(b, 0, 1, False)
    compute(1, 2 * c + 1)


def _nsa_rows(qa, ds):
    low = lax.broadcasted_iota(jnp.int32, (ds, LANES), 1) < HEAD_DIM_A
    return jnp.concatenate([jnp.where(low if k == 0 else ~low, qa[:, g * LANES:(g + 1) * LANES], 0.0)
                            for k in range(N_KV_A) for g in range(GROUP_A)], axis=0)


def _cmp_pool_kernel(pt_ref, kpool, vpool, qa_ref, cc_ref, scl_ref, sch_ref, wpk_ref, wpv_ref,
                     ocmp_ref, score_ref, kbuf, vbuf, sem, kc_s, vc_s, *, n):
    blocks = PAGE_SIZE // CMP_BLOCK
    rows = n * blocks

    def compute(slot, chunk):
        r0 = pl.multiple_of(chunk * rows, rows)

        def means(buf):
            out = []
            for i in range(n):
                xt = buf[slot, i].T
                out.append(jnp.mean(xt.reshape(blocks, CMP_BLOCK, KV_W), axis=1))
            return jnp.concatenate(out, axis=0).astype(jnp.bfloat16)

        kc = jnp.dot(means(kbuf), wpk_ref[...], preferred_element_type=jnp.float32)
        tab = lambda t: t[pl.ds(r0, rows), :]
        kc_s[pl.ds(r0, rows), :] = _rope(kc, tab(cc_ref), tab(scl_ref), tab(sch_ref), ROT_DIM_A // 2)
        vc_s[pl.ds(r0, rows), :] = jnp.dot(means(vbuf), wpv_ref[...], preferred_element_type=jnp.float32)

    _paged_pipeline(pt_ref, (kpool, vpool), (kbuf, vbuf), sem, n, compute)

    @pl.when(pl.program_id(1) == pl.num_programs(1) - 1)
    def _():
        ds = qa_ref.shape[1]
        qb = _nsa_rows(qa_ref[0], ds).astype(jnp.bfloat16)
        s = lax.dot_general(qb, kc_s[...].astype(jnp.bfloat16), _NT, preferred_element_type=jnp.float32)
        p = jnp.exp(s - jnp.max(s, axis=-1, keepdims=True))
        p = p / jnp.sum(p, axis=-1, keepdims=True)
        ocmp_ref[0] = jnp.dot(p.astype(jnp.bfloat16), vc_s[...].astype(jnp.bfloat16), preferred_element_type=jnp.float32)
        gd = GROUP_A * ds
        imp = jnp.concatenate([sum(p[k * gd + g * ds:k * gd + (g + 1) * ds] for g in range(GROUP_A))
                               for k in range(N_KV_A)], axis=0)
        w = imp.shape[1]
        even = lax.broadcasted_iota(jnp.int32, imp.shape, 1) % 2 == 0
        score_ref[0] = jnp.where(even, imp + pltpu.roll(imp, w - 1, 1), pltpu.roll(imp, 1, 1) + imp)


def _pool_t(cache):
    n_phys = cache.shape[1]
    return jnp.transpose(cache[0], (0, 2, 3, 1)).reshape(n_phys, KV_W, PAGE_SIZE)


def _cmp_pool(page_table, ps, k_pool_t, v_pool_t, wts, db, ds):
    n_pages = page_table.shape[1]
    n = PAGES_PER_CHUNK
    blocks = PAGE_SIZE // CMP_BLOCK
    n_cmp = n_pages * blocks
    cmp_pos = jnp.arange(n_cmp, dtype=jnp.int32) * CMP_BLOCK + (CMP_BLOCK - 1)
    tc = _rope_tables(cmp_pos, ROT_DIM_A, HEAD_DIM_A)
    f32 = jnp.float32
    const = lambda shape: pl.BlockSpec(shape, lambda b, c, pt: (0,) * len(shape))
    per_b = lambda r, w: pl.BlockSpec((1, r, w), lambda b, c, pt: (b, 0, 0))
    hbm = pl.BlockSpec(memory_space=pl.ANY)
    rows = N_KV_A * GROUP_A * ds
    page = (KV_W, PAGE_SIZE)
    return pl.pallas_call(
        functools.partial(_cmp_pool_kernel, n=n),
        grid_spec=pltpu.PrefetchScalarGridSpec(
            num_scalar_prefetch=1, grid=(db, n_pages // (2 * n)),
            in_specs=[hbm, hbm, per_b(ds, WIDTH_A)] + [const((n_cmp, LANES))] * 3 + [const((KV_W, KV_W))] * 2,
            out_specs=[per_b(rows, LANES), per_b(N_KV_A * ds, n_cmp)],
            scratch_shapes=[pltpu.VMEM((2, n) + page, f32), pltpu.VMEM((2, n) + page, f32),
                            pltpu.SemaphoreType.DMA((2, 2)),
                            pltpu.VMEM((n_cmp, LANES), f32), pltpu.VMEM((n_cmp, LANES), f32)]),
        out_shape=[jax.ShapeDtypeStruct((db, rows, LANES), f32), jax.ShapeDtypeStruct((db, N_KV_A * ds, n_cmp), f32)],
        compiler_params=pltpu.CompilerParams(dimension_semantics=("arbitrary", "arbitrary")),
        name="cmp_pool",
    )(page_table, k_pool_t, v_pool_t, ps['qa'].astype(f32).reshape(db, ds, WIDTH_A), *tc, wts['wpk'], wts['wpv'])


def _rank_kernel(sc_ref, sel_ref, sc_s, *, n_keep):
    nblk = sc_ref.shape[0]
    row = lax.broadcasted_iota(jnp.int32, sc_ref.shape, 0)
    sc = jnp.where((row == 0) | (row == nblk - 1), jnp.inf, sc_ref[...])
    sc_s[...] = sc

    def body(i, cnt):
        other = sc_s[pl.ds(i, 1), :]
        ge = (other >= sc).astype(jnp.int32)
        gt = (other > sc).astype(jnp.int32)
        return cnt + jnp.where(row > i, ge, gt)

    cnt = lax.fori_loop(0, nblk, body, jnp.zeros(sc_ref.shape, jnp.int32), unroll=8)
    sel_ref[...] = (cnt < n_keep).astype(jnp.float32)


def _rank_blocks(score_t, n_keep):
    nblk, rows = score_t.shape
    spec = pl.BlockSpec((nblk, LANES), lambda i: (0, i))
    return pl.pallas_call(
        functools.partial(_rank_kernel, n_keep=n_keep),
        grid=(rows // LANES,), in_specs=[spec], out_specs=spec,
        out_shape=jax.ShapeDtypeStruct((nblk, rows), jnp.float32),
        scratch_shapes=[pltpu.VMEM((nblk, LANES), jnp.float32)],
        compiler_params=pltpu.CompilerParams(dimension_semantics=("arbitrary",)),
        name="rank_blocks",
    )(score_t)


def _new_token_scores(q, k_new, t_row, ds):
    cols, masks = [], []
    for s in range(ds):
        cols.append(jnp.sum(q * k_new[s:s + 1, :], axis=-1, keepdims=True))
        masks.append(t_row >= s)
    return cols, masks


def _merge_new_tokens(m, l, acc, cols, masks, v_new):
    m_new = m
    for c, k in zip(cols, masks):
        m_new = jnp.maximum(m_new, jnp.where(k, c, NEG))
    alpha = jnp.exp(m - m_new)
    l, acc = alpha * l, alpha * acc
    for s, (c, k) in enumerate(zip(cols, masks)):
        p = jnp.where(k, jnp.exp(c - m_new), 0.0)
        l = l + p
        acc = acc + p.astype(jnp.bfloat16).astype(jnp.float32) * v_new[s:s + 1, :]
    return m_new, l, acc


def _nsa_sample_kernel(pt_ref, kpool, vpool, qa_ref, new_ref, ocmp_ref, sel_ref, exp_ref, wk_ref, wv_ref, g_ref, sza_ref,
                       out_ref, kbuf, vbuf, sem, m_ref, l_ref, acc_ref, owin_ref, *, n):
    c = pl.program_id(1)
    ds = qa_ref.shape[1]
    rows = N_KV_A * GROUP_A * ds
    low = lax.broadcasted_iota(jnp.int32, (ds, LANES), 1) < HEAD_DIM_A
    q = _nsa_rows(qa_ref[0], ds)
    qb = q.astype(jnp.bfloat16)
    t_row = lax.broadcasted_iota(jnp.int32, (rows, 1), 0) % ds
    new = new_ref[0]

    @pl.when(c == 0)
    def _():
        kw = wk_ref[0].astype(jnp.bfloat16)
        vw = wv_ref[0].astype(jnp.bfloat16)
        s = jnp.dot(qb, kw, preferred_element_type=jnp.float32)
        wb = s.shape[1]
        idx = lax.broadcasted_iota(jnp.int32, (rows, wb), 1)
        mask = idx > t_row + (wb - WINDOW)
        m = jnp.max(jnp.where(mask, s, NEG), axis=-1, keepdims=True)
        p = jnp.where(mask, jnp.exp(s - m), 0.0)
        l = jnp.sum(p, axis=-1, keepdims=True)
        acc = lax.dot_general(p.astype(jnp.bfloat16), vw, _NT, preferred_element_type=jnp.float32)
        cols, masks = _new_token_scores(q, new[:, 2 * KV_W:3 * KV_W], t_row, ds)
        m, l, acc = _merge_new_tokens(m, l, acc, cols, masks, new[:, 3 * KV_W:4 * KV_W])
        owin_ref[...] = acc / l

        cols, masks = _new_token_scores(q, new[:, 0:KV_W], t_row, ds)
        m0 = jnp.full((rows, 1), NEG, jnp.float32)
        m, l, acc = _merge_new_tokens(m0, jnp.zeros((rows, 1), jnp.float32), jnp.zeros((rows, LANES), jnp.float32),
                                      cols, masks, new[:, KV_W:2 * KV_W])
        m_ref[...], l_ref[...], acc_ref[...] = m, l, acc

    def compute(slot, chunk):
        sel = sel_ref[0, slot]
        sel_rows = jnp.concatenate([sel[k * ds:(k + 1) * ds] for k in range(N_KV_A) for _ in range(GROUP_A)], axis=0)
        mask = jnp.dot(sel_rows.astype(jnp.bfloat16), exp_ref[...], preferred_element_type=jnp.float32) > 0.5
        s = jnp.concatenate([jnp.dot(qb, kbuf[slot, i].astype(jnp.bfloat16), preferred_element_type=jnp.float32)
                             for i in range(n)], axis=1)
        m_old = m_ref[...]
        m_new = jnp.maximum(m_old, jnp.max(jnp.where(mask, s, NEG), axis=-1, keepdims=True))
        p = jnp.where(mask, jnp.exp(s - m_new), 0.0)
        alpha = jnp.exp(m_old - m_new)
        l_ref[...] = alpha * l_ref[...] + jnp.sum(p, axis=-1, keepdims=True)
        pb = p.astype(jnp.bfloat16)
        pv = sum(lax.dot_general(pb[:, i * PAGE_SIZE:(i + 1) * PAGE_SIZE], vbuf[slot, i].astype(jnp.bfloat16), _NT,
                                 preferred_element_type=jnp.float32) for i in range(n))
        acc_ref[...] = alpha * acc_ref[...] + pv
        m_ref[...] = m_new

    _paged_pipeline(pt_ref, (kpool, vpool), (kbuf, vbuf), sem, n, compute)

    @pl.when(c == pl.num_programs(1) - 1)
    def _():
        osel = acc_ref[...] / l_ref[...]
        gd = GROUP_A * ds
        for g in range(GROUP_A):
            pick = lambda o: jnp.where(low, o[g * ds:(g + 1) * ds], o[gd + g * ds:gd + (g + 1) * ds])
            cs = slice(g * LANES, (g + 1) * LANES)
            o = (g_ref[0, :, cs] * pick(ocmp_ref[0])
                 + g_ref[0, :, WIDTH_A + g * LANES:WIDTH_A + (g + 1) * LANES] * pick(osel)
                 + g_ref[0, :, 2 * WIDTH_A + g * LANES:2 * WIDTH_A + (g + 1) * LANES] * pick(owin_ref[...]))
            out_ref[0, :, cs] = o * sza_ref[0, :, cs]


def _select_past_blocks(score, db, ds, n):
    rows = score.shape[1]
    score_t = score[:, :, 0::2].reshape(db * rows, -1).T
    sel_t = _rank_blocks(score_t, N_SEL - 1)
    per_chunk = n * PAGE_SIZE // SEL_BLOCK
    sel = sel_t.T.reshape(db, rows, -1, per_chunk).transpose(0, 2, 1, 3)
    return jnp.pad(sel, ((0, 0), (0, 0), (0, 0), (0, LANES - per_chunk)))


def _nsa_sample(page_table, ps, ocmp, sel, k_pool_t, v_pool_t, win_k_t, win_v_t, db, ds):
    n = PAGES_PER_CHUNK
    n_pages = page_table.shape[1]
    wb = win_k_t.shape[2]
    f32 = jnp.float32
    per_b = lambda r, w: pl.BlockSpec((1, r, w), lambda b, c, pt: (b, 0, 0))
    hbm = pl.BlockSpec(memory_space=pl.ANY)
    rows = N_KV_A * GROUP_A * ds
    width = n * PAGE_SIZE
    expand = jnp.asarray(np.arange(LANES)[:, None] == np.arange(width)[None, :] // SEL_BLOCK, jnp.bfloat16)
    args = (ps['qa'].astype(f32).reshape(db, ds, WIDTH_A), ps['kvb'].astype(f32).reshape(db, ds, 4 * KV_W),
            ocmp, sel, expand, win_k_t, win_v_t, ps['gates'].reshape(db, ds, 3 * WIDTH_A), ps['sza'].reshape(db, ds, WIDTH_A))
    in_specs = [hbm, hbm, per_b(ds, WIDTH_A), per_b(ds, 4 * KV_W), per_b(rows, LANES),
                pl.BlockSpec((1, 2, N_KV_A * ds, LANES), lambda b, c, pt: (b, c, 0, 0)),
                pl.BlockSpec((LANES, width), lambda b, c, pt: (0, 0)),
                per_b(KV_W, wb), per_b(KV_W, wb), per_b(ds, 3 * WIDTH_A), per_b(ds, WIDTH_A)]
    page = (KV_W, PAGE_SIZE)
    return pl.pallas_call(
        functools.partial(_nsa_sample_kernel, n=n),
        grid_spec=pltpu.PrefetchScalarGridSpec(
            num_scalar_prefetch=1, grid=(db, n_pages // (2 * n)), in_specs=in_specs,
            out_specs=per_b(ds, WIDTH_A),
            scratch_shapes=[pltpu.VMEM((2, n) + page, f32), pltpu.VMEM((2, n) + page, f32),
                            pltpu.SemaphoreType.DMA((2, 2)),
                            pltpu.VMEM((rows, 1), f32), pltpu.VMEM((rows, 1), f32), pltpu.VMEM((rows, LANES), f32),
                            pltpu.VMEM((rows, LANES), f32)]),
        out_shape=jax.ShapeDtypeStruct((db, ds, WIDTH_A), f32),
        compiler_params=pltpu.CompilerParams(dimension_semantics=("arbitrary", "arbitrary")),
        name="nsa_sample",
    )(page_table, k_pool_t, v_pool_t, *args)


def _mla_sample_kernel(pt_ref, ckv_pool, kpe_pool, q_ref, new_ref, szb_ref, wuv_ref, out_ref,
                       ckv_buf, kpe_buf, sem, m_ref, l_ref, acc_ref, *, n):
    c = pl.program_id(1)
    ds = q_ref.shape[1]
    rows = N_HEADS_B * ds
    qw = 2 * LANES
    q = jnp.concatenate([q_ref[0, :, h * qw:(h + 1) * qw] for h in range(N_HEADS_B)], axis=0)
    q_abs = q[:, :KV_RANK].astype(jnp.bfloat16)
    q_pe = q[:, KV_RANK:].astype(jnp.bfloat16)
    t_row = lax.broadcasted_iota(jnp.int32, (rows, 1), 0) % ds

    @pl.when(c == 0)
    def _():
        new = new_ref[0]
        cols, masks = _new_token_scores(q, new, t_row, ds)
        cols = [x * MLA_SCALE for x in cols]
        m0 = jnp.full((rows, 1), NEG, jnp.float32)
        m, l, acc = _merge_new_tokens(m0, jnp.zeros((rows, 1), jnp.float32), jnp.zeros((rows, KV_RANK), jnp.float32),
                                      cols, masks, new[:, :KV_RANK])
        m_ref[...], l_ref[...], acc_ref[...] = m, l, acc

    pad = jnp.zeros((LANES - ROPE_DIM_B, PAGE_SIZE), jnp.float32)

    def compute(slot, chunk):
        ckvs = [ckv_buf[slot, i].astype(jnp.bfloat16) for i in range(n)]
        s = jnp.concatenate(
            [lax.dot_general(q_abs, ckvs[i], _NT, preferred_element_type=jnp.float32)
             + jnp.dot(q_pe, jnp.concatenate([kpe_buf[slot, i], pad], axis=0).astype(jnp.bfloat16),
                       preferred_element_type=jnp.float32)
             for i in range(n)], axis=1) * MLA_SCALE
        m_old = m_ref[...]
        m_new = jnp.maximum(m_old, jnp.max(s, axis=-1, keepdims=True))
        p = jnp.exp(s - m_new)
        alpha = jnp.exp(m_old - m_new)
        l_ref[...] = alpha * l_ref[...] + jnp.sum(p, axis=-1, keepdims=True)
        pb = p.astype(jnp.bfloat16)
        pv = sum(jnp.dot(pb[:, i * PAGE_SIZE:(i + 1) * PAGE_SIZE], kv, preferred_element_type=jnp.float32)
                 for i, kv in enumerate(ckvs))
        acc_ref[...] = alpha * acc_ref[...] + pv
        m_ref[...] = m_new

    _paged_pipeline(pt_ref, (ckv_pool, kpe_pool), (ckv_buf, kpe_buf), sem, n, compute)

    @pl.when(c == pl.num_programs(1) - 1)
    def _():
        o = acc_ref[...] / l_ref[...]
        wide = jnp.concatenate([o[h * ds:(h + 1) * ds] for h in range(N_HEADS_B)], axis=1)
        wide = jnp.concatenate([wide, jnp.zeros_like(wide)], axis=0).astype(jnp.bfloat16)
        ob = jnp.dot(wide, wuv_ref[...], preferred_element_type=jnp.float32)[:ds]
        out_ref[0] = ob * szb_ref[0]


def _mla_sample(page_table, ps, ckv_pool, kpe_pool_t, wts, db, ds):
    n = PAGES_PER_CHUNK
    n_pages = page_table.shape[1]
    f32 = jnp.float32
    per_b = lambda r, w: pl.BlockSpec((1, r, w), lambda b, c, pt: (b, 0, 0))
    hbm = pl.BlockSpec(memory_space=pl.ANY)
    rows = N_HEADS_B * ds
    args = (ps['qmla'].astype(f32).reshape(db, ds, N_HEADS_B * 2 * LANES), ps['kmla'].astype(f32).reshape(db, ds, 2 * KV_RANK),
            ps['szb'].reshape(db, ds, WIDTH_B), wts['wuv'])
    in_specs = [hbm, hbm, per_b(ds, N_HEADS_B * 2 * LANES), per_b(ds, 2 * KV_RANK), per_b(ds, WIDTH_B),
                pl.BlockSpec(wts['wuv'].shape, lambda b, c, pt: (0, 0))]
    return pl.pallas_call(
        functools.partial(_mla_sample_kernel, n=n),
        grid_spec=pltpu.PrefetchScalarGridSpec(
            num_scalar_prefetch=1, grid=(db, n_pages // (2 * n)), in_specs=in_specs,
            out_specs=per_b(ds, WIDTH_B),
            scratch_shapes=[pltpu.VMEM((2, n, PAGE_SIZE, KV_RANK), f32), pltpu.VMEM((2, n, ROPE_DIM_B, PAGE_SIZE), f32),
                            pltpu.SemaphoreType.DMA((2, 2)),
                            pltpu.VMEM((rows, 1), f32), pltpu.VMEM((rows, 1), f32), pltpu.VMEM((rows, KV_RANK), f32)]),
        out_shape=jax.ShapeDtypeStruct((db, ds, WIDTH_B), f32),
        compiler_params=pltpu.CompilerParams(dimension_semantics=("arbitrary", "arbitrary")),
        name="mla_sample",
    )(page_table, ckv_pool, kpe_pool_t, *args)


def _kv_heads(x, lead):
    return x.reshape(*lead, N_KV_A, HEAD_DIM_A)


def kernel(x_prompt, x_sample, cache_k_cmp, cache_v_cmp, cache_k_sel, cache_v_sel, cache_ckv, cache_kpe, state_win_k, state_win_v, page_table, ln_in_g, ln_in_b, w_in, w_phi_k, w_phi_v, q_norm_g, w_uq, kv_norm_g, w_uk, w_uv, w_o, ln_g, ln_b):
    depth = w_in.shape[0]
    assert depth == 1, "one layer only"
    alpha = (2.0 * depth) ** 0.25
    b, seq, _ = x_prompt.shape
    db, ds, _ = x_sample.shape
    past = page_table.shape[1] * PAGE_SIZE
    l = 0
    wts = _prep_weights(w_in[l], w_phi_k[l], w_phi_v[l], w_uq[l], w_uk[l], w_uv[l], w_o[l])

    xp = x_prompt.reshape(b * seq, D_MODEL)
    pos_p = jnp.arange(seq, dtype=jnp.int32)
    cmp_pos = jnp.arange(seq // CMP_BLOCK, dtype=jnp.int32) * CMP_BLOCK + (CMP_BLOCK - 1)
    pp = _project(xp, pos_p, cmp_pos, wts, ln_in_g, ln_in_b, q_norm_g[l], kv_norm_g[l], tm=256, with_cmp=True)
    mixed_a = _nsa_prompt(pp, b, seq)
    mixed_b = _mla_prompt(pp, wts, b, seq)
    y_p = _out_project(xp, mixed_a, mixed_b, wts, ln_in_g, ln_in_b, ln_g[l], ln_b[l], tm=256, alpha=alpha)

    kvp = pp['kv'].reshape(b, seq, 6, N_KV_A, HEAD_DIM_A)
    wl = min(WINDOW, seq)
    p_out = [kvp[None, :, :, i] for i in range(4)]
    p_ckv = pp['ckv'].reshape(1, b, seq, KV_RANK)
    p_kpe = pp['kpe'].reshape(b, seq, LANES)[None, :, :, :ROPE_DIM_B]
    p_wk, p_wv = kvp[None, :, seq - wl:, 4], kvp[None, :, seq - wl:, 5]

    xs = x_sample.reshape(db * ds, D_MODEL)
    tm_s = 256
    pos_s = jnp.tile(past + jnp.arange(ds, dtype=jnp.int32), tm_s // ds)
    ps = _project(xs, pos_s, jnp.zeros((tm_s // CMP_BLOCK,), jnp.int32), wts, ln_in_g, ln_in_b, q_norm_g[l], kv_norm_g[l],
                  tm=tm_s, with_cmp=False)
    ocmp_s, score_s = _cmp_pool(page_table, ps, _pool_t(cache_k_cmp), _pool_t(cache_v_cmp), wts, db, ds)
    sel_s = _select_past_blocks(score_s, db, ds, PAGES_PER_CHUNK)
    win_t = lambda st: jnp.transpose(st[l], (0, 2, 3, 1)).reshape(db, KV_W, st.shape[2])
    mixed_a_s = _nsa_sample(page_table, ps, ocmp_s, sel_s, _pool_t(cache_k_sel), _pool_t(cache_v_sel),
                            win_t(state_win_k), win_t(state_win_v), db, ds)
    kpe_pool_t = jnp.transpose(cache_kpe[l], (0, 2, 1))
    mixed_b_s = _mla_sample(page_table, ps, cache_ckv[l], kpe_pool_t, wts, db, ds)
    y_s = _out_project(xs, mixed_a_s.reshape(db * ds, WIDTH_A).astype(jnp.bfloat16),
                       mixed_b_s.reshape(db * ds, WIDTH_B).astype(jnp.bfloat16),
                       wts, ln_in_g, ln_in_b, ln_g[l], ln_b[l], tm=tm_s, alpha=alpha)

    kvs = ps['kv'].reshape(db, ds, 6, N_KV_A, HEAD_DIM_A)
    s_out = [kvs[None, :, :, i] for i in range(4)]
    s_ckv = ps['ckv'].reshape(1, db, ds, KV_RANK)
    s_kpe = ps['kpe'].reshape(db, ds, LANES)[None, :, :, :ROPE_DIM_B]
    wb = state_win_k.shape[2]
    s_wk = jnp.concatenate([state_win_k[l], kvs[:, :, 4]], axis=1)[None, :, -wb:]
    s_wv = jnp.concatenate([state_win_v[l], kvs[:, :, 5]], axis=1)[None, :, -wb:]
    return (y_p.reshape(b, seq, D_MODEL), y_s.reshape(db, ds, D_MODEL), *p_out, p_ckv, p_kpe, p_wk, p_wv,
            *s_out, s_ckv, s_kpe, s_wk, s_wv)
```

```python
import functools

import numpy as np
import jax
import jax.numpy as jnp
from jax import lax
from jax.experimental import pallas as pl
from jax.experimental.pallas import tpu as pltpu

D_MODEL = 1024
HEAD_DIM_A = 64
N_HEADS_A = 8
N_KV_A = 2
GROUP_A = 4
ROT_DIM_A = 16
CMP_BLOCK = 32
SEL_BLOCK = 64
N_SEL = 16
WINDOW = 512
N_HEADS_B = 8
NOPE_DIM = 64
ROPE_DIM_B = 32
V_DIM = 64
Q_RANK = 256
KV_RANK = 128
ROPE_THETA = 500000.0
LN_EPS = 1e-5
RMS_EPS = 1e-6
PAGE_SIZE = 128
WIDTH_A = 512
WIDTH_B = 512
KV_W = 128
SCALE_A = HEAD_DIM_A ** -0.5
MLA_SCALE = (NOPE_DIM + ROPE_DIM_B) ** -0.5
IN_WIDTH = 2744

LANES = 128
NEG = -1e30

OFF_QA, OFF_KV, OFF_GATE, OFF_ZA, OFF_CQ, OFF_CKV, OFF_KPE, OFF_ZB = 0, 512, 1280, 1304, 1816, 2072, 2200, 2232

SEG_QA, SEG_KV, SEG_G, SEG_ZA, SEG_CQ, SEG_CKV, SEG_KPE, SEG_ZB = 0, 512, 1280, 2816, 3328, 3584, 3712, 3840
EXT_WIDTH = 4352

_NT = (((1,), (1,)), ((), ()))


def _paired_head(lane):
    g, s = lane // LANES, (lane % LANES) // HEAD_DIM_A
    return g + GROUP_A * s, lane % HEAD_DIM_A


def _proj_row_index():
    idx = np.full((EXT_WIDTH,), IN_WIDTH, np.int32)
    for lane in range(WIDTH_A):
        h, d = _paired_head(lane)
        idx[SEG_QA + lane] = OFF_QA + h * HEAD_DIM_A + d
        idx[SEG_ZA + lane] = OFF_ZA + h * HEAD_DIM_A + d
        for j in range(3):
            idx[SEG_G + j * WIDTH_A + lane] = OFF_GATE + h * 3 + j
    idx[SEG_KV:SEG_KV + 6 * KV_W] = OFF_KV + np.arange(6 * KV_W)
    idx[SEG_CQ:SEG_CQ + Q_RANK] = OFF_CQ + np.arange(Q_RANK)
    idx[SEG_CKV:SEG_CKV + KV_RANK] = OFF_CKV + np.arange(KV_RANK)
    idx[SEG_KPE:SEG_KPE + ROPE_DIM_B] = OFF_KPE + np.arange(ROPE_DIM_B)
    idx[SEG_ZB:SEG_ZB + WIDTH_B] = OFF_ZB + np.arange(WIDTH_B)
    return idx


def _rope_tables(pos, rot_dim, period):
    half = rot_dim // 2
    inv = ROPE_THETA ** (-jnp.arange(half, dtype=jnp.float32) * 2.0 / rot_dim)
    ang = pos.astype(jnp.float32)[:, None] * inv
    cos, sin = jnp.cos(ang), jnp.sin(ang)
    n = pos.shape[0]
    pad = jnp.zeros((n, period - rot_dim), jnp.float32)
    c = jnp.concatenate([cos, cos, pad + 1.0], axis=1)
    s_lo = jnp.concatenate([-sin, jnp.zeros_like(sin), pad], axis=1)
    s_hi = jnp.concatenate([jnp.zeros_like(sin), sin, pad], axis=1)
    rep = LANES // period
    return tuple(jnp.tile(t, (1, rep)) for t in (c, s_lo, s_hi))


def _rope(x, c, s_lo, s_hi, half):
    w = x.shape[1]
    return x * c + pltpu.roll(x, w - half, 1) * s_lo + pltpu.roll(x, half, 1) * s_hi


def _tile_lanes(t, width):
    return t if width == LANES else jnp.concatenate([t] * (width // LANES), axis=1)


def _layernorm(x, g, b):
    mu = jnp.mean(x, axis=-1, keepdims=True)
    xc = x - mu
    var = jnp.mean(xc * xc, axis=-1, keepdims=True)
    return xc * lax.rsqrt(var + LN_EPS) * g + b


def _rmsnorm(x, g):
    return x * lax.rsqrt(jnp.mean(x * x, axis=-1, keepdims=True) + RMS_EPS) * g


def _sigmoid(x):
    return 1.0 / (1.0 + jnp.exp(-x))


def _proj_kernel(x_ref, lng_ref, lnb_ref, w_ref, qng_ref, wuq_ref, wuk_ref, kvng_ref,
                 ca_ref, sal_ref, sah_ref, cb_ref, sbl_ref, sbh_ref,
                 cc_ref, scl_ref, sch_ref, wpk_ref, wpv_ref,
                 qa_ref, kv_ref, kvb_ref, g_ref, sza_ref, szb_ref, ckv_ref, kpe_ref, kmla_ref, qmla_ref,
                 kc_ref, vc_ref, *, with_cmp):
    hn = _layernorm(x_ref[...], lng_ref[...], lnb_ref[...]).astype(jnp.bfloat16)

    def seg(lo, width):
        return lax.dot_general(hn, w_ref[lo:lo + width, :], _NT, preferred_element_type=jnp.float32)

    ca, sal, sah = ca_ref[...], sal_ref[...], sah_ref[...]
    half_a = ROT_DIM_A // 2

    qa = seg(SEG_QA, WIDTH_A)
    qa = _rope(qa, _tile_lanes(ca, WIDTH_A), _tile_lanes(sal, WIDTH_A), _tile_lanes(sah, WIDTH_A), half_a)
    qa_ref[...] = (qa * SCALE_A).astype(jnp.bfloat16)

    kv = seg(SEG_KV, 6 * KV_W)
    k_cmp, v_cmp = kv[:, 0:KV_W], kv[:, KV_W:2 * KV_W]
    k_sel = _rope(kv[:, 2 * KV_W:3 * KV_W], ca, sal, sah, half_a)
    v_sel = kv[:, 3 * KV_W:4 * KV_W]
    k_win = _rope(kv[:, 4 * KV_W:5 * KV_W], ca, sal, sah, half_a)
    v_win = kv[:, 5 * KV_W:6 * KV_W]
    kv_ref[:, 0:2 * KV_W] = kv[:, 0:2 * KV_W]
    kv_ref[:, 2 * KV_W:3 * KV_W] = k_sel
    kv_ref[:, 3 * KV_W:4 * KV_W] = v_sel
    kv_ref[:, 4 * KV_W:5 * KV_W] = k_win
    kv_ref[:, 5 * KV_W:6 * KV_W] = v_win
    kvb_ref[:, 0:KV_W] = k_sel.astype(jnp.bfloat16)
    kvb_ref[:, KV_W:2 * KV_W] = v_sel.astype(jnp.bfloat16)
    kvb_ref[:, 2 * KV_W:3 * KV_W] = k_win.astype(jnp.bfloat16)
    kvb_ref[:, 3 * KV_W:4 * KV_W] = v_win.astype(jnp.bfloat16)

    for j in range(3):
        g_ref[:, j * WIDTH_A:(j + 1) * WIDTH_A] = _sigmoid(seg(SEG_G + j * WIDTH_A, WIDTH_A))

    za = seg(SEG_ZA, WIDTH_A)
    sza_ref[...] = za * _sigmoid(za)
    zb = seg(SEG_ZB, WIDTH_B)
    szb_ref[...] = zb * _sigmoid(zb)

    ckv = _rmsnorm(seg(SEG_CKV, KV_RANK), kvng_ref[...])
    ckv_ref[...] = ckv
    half_b = ROPE_DIM_B // 2
    cb, sbl, sbh = cb_ref[...], sbl_ref[...], sbh_ref[...]
    kpe = _rope(seg(SEG_KPE, LANES), cb, sbl, sbh, half_b)
    kpe_ref[...] = kpe
    kmla_ref[:, 0:KV_RANK] = ckv.astype(jnp.bfloat16)
    kmla_ref[:, KV_RANK:2 * KV_RANK] = kpe.astype(jnp.bfloat16)

    cq = _rmsnorm(seg(SEG_CQ, Q_RANK), qng_ref[...]).astype(jnp.bfloat16)
    qb = jnp.dot(cq, wuq_ref[...], preferred_element_type=jnp.float32)
    nope_w = N_HEADS_B * NOPE_DIM
    q_abs = jnp.dot(qb[:, :nope_w].astype(jnp.bfloat16), wuk_ref[...], preferred_element_type=jnp.float32)
    pe_w = N_HEADS_B * LANES
    q_pe = _rope(qb[:, nope_w:], _tile_lanes(cb, pe_w), _tile_lanes(sbl, pe_w), _tile_lanes(sbh, pe_w), half_b)
    for h in range(N_HEADS_B):
        qmla_ref[:, 2 * h * LANES:(2 * h + 1) * LANES] = q_abs[:, h * LANES:(h + 1) * LANES].astype(jnp.bfloat16)
        qmla_ref[:, (2 * h + 1) * LANES:(2 * h + 2) * LANES] = q_pe[:, h * LANES:(h + 1) * LANES].astype(jnp.bfloat16)

    if with_cmp:
        tm = k_cmp.shape[0]
        nblk = tm // CMP_BLOCK
        k_mean = jnp.mean(k_cmp.reshape(nblk, CMP_BLOCK, KV_W), axis=1).astype(jnp.bfloat16)
        v_mean = jnp.mean(v_cmp.reshape(nblk, CMP_BLOCK, KV_W), axis=1).astype(jnp.bfloat16)
        kc = jnp.dot(k_mean, wpk_ref[...], preferred_element_type=jnp.float32)
        kc_ref[...] = _rope(kc, cc_ref[...], scl_ref[...], sch_ref[...], half_a)
        vc_ref[...] = jnp.dot(v_mean, wpv_ref[...], preferred_element_type=jnp.float32)
    else:
        kc_ref[...] = jnp.zeros_like(kc_ref)
        vc_ref[...] = jnp.zeros_like(vc_ref)


def _block_diag2(w):
    z = jnp.zeros((HEAD_DIM_A, HEAD_DIM_A), w.dtype)
    return jnp.concatenate([jnp.concatenate([w[0], z], 1), jnp.concatenate([z, w[1]], 1)], 0)


def _prep_weights(w_in, w_phi_k, w_phi_v, w_uq, w_uk, w_uv, w_o):
    wt = jnp.swapaxes(w_in, 0, 1)
    wt = jnp.concatenate([wt, jnp.zeros((1, D_MODEL), wt.dtype)], axis=0)
    w_ext = jnp.take(wt, jnp.asarray(_proj_row_index()), axis=0).astype(jnp.bfloat16)
    nope = w_uq[:, :, :NOPE_DIM].reshape(Q_RANK, N_HEADS_B * NOPE_DIM)
    pe = jnp.pad(w_uq[:, :, NOPE_DIM:], ((0, 0), (0, 0), (0, LANES - ROPE_DIM_B))).reshape(Q_RANK, N_HEADS_B * LANES)
    wuq = jnp.concatenate([nope, pe], axis=1).astype(jnp.bfloat16)
    eye = jnp.eye(N_HEADS_B, dtype=w_uk.dtype)
    wuk = jnp.einsum('rhd,hg->hdgr', w_uk, eye).reshape(N_HEADS_B * NOPE_DIM, N_HEADS_B * KV_RANK).astype(jnp.bfloat16)
    wuv = jnp.einsum('rhd,hg->hrgd', w_uv, eye).reshape(N_HEADS_B * KV_RANK, N_HEADS_B * V_DIM).astype(jnp.bfloat16)
    heads, dims = _paired_head(np.arange(WIDTH_A))
    woa = jnp.take(w_o[:WIDTH_A], jnp.asarray(heads * HEAD_DIM_A + dims), axis=0).astype(jnp.bfloat16)
    wob = w_o[WIDTH_A:].astype(jnp.bfloat16)
    wpk = _block_diag2(w_phi_k).astype(jnp.bfloat16)
    wpv = _block_diag2(w_phi_v).astype(jnp.bfloat16)
    return dict(w_ext=w_ext, wuq=wuq, wuk=wuk, wuv=wuv, woa=woa, wob=wob, wpk=wpk, wpv=wpv)


def _project(x, pos_rows, cmp_pos, wts, ln_in_g, ln_in_b, q_norm_g, kv_norm_g, *, tm, with_cmp):
    m = x.shape[0]
    n_tiles = m // tm
    ta = _rope_tables(pos_rows, ROT_DIM_A, HEAD_DIM_A)
    tb = _rope_tables(pos_rows, ROPE_DIM_B, LANES)
    tc = _rope_tables(cmp_pos, ROT_DIM_A, HEAD_DIM_A)
    tab_tiles = pos_rows.shape[0] // tm
    nblk = tm // CMP_BLOCK
    cmp_tiles = max(cmp_pos.shape[0] // nblk, 1)

    row = lambda w: pl.BlockSpec((tm, w), lambda i: (i, 0))
    full = lambda a: pl.BlockSpec(a.shape, lambda i: (0,) * a.ndim)
    tab = pl.BlockSpec((tm, LANES), lambda i: (i % tab_tiles, 0))
    ctab = pl.BlockSpec((nblk, LANES), lambda i: (i % cmp_tiles, 0))
    vec = lambda v: v.reshape(1, -1)
    args = (x, vec(ln_in_g), vec(ln_in_b), wts['w_ext'], vec(q_norm_g), wts['wuq'], wts['wuk'], vec(kv_norm_g),
            *ta, *tb, *tc, wts['wpk'], wts['wpv'])
    in_specs = [row(D_MODEL)] + [full(a) for a in args[1:8]] + [tab] * 6 + [ctab] * 3 + [full(wts['wpk']), full(wts['wpv'])]
    f32, bf16 = jnp.float32, jnp.bfloat16
    outs = [(WIDTH_A, bf16), (6 * KV_W, f32), (4 * KV_W, bf16), (3 * WIDTH_A, f32), (WIDTH_A, f32), (WIDTH_B, f32),
            (KV_RANK, f32), (LANES, f32), (2 * KV_RANK, bf16), (N_HEADS_B * 2 * LANES, bf16)]
    out_shape = [jax.ShapeDtypeStruct((m, w), dt) for w, dt in outs]
    out_specs = [row(w) for w, _ in outs]
    out_shape += [jax.ShapeDtypeStruct((n_tiles * nblk, LANES), f32)] * 2
    out_specs += [pl.BlockSpec((nblk, LANES), lambda i: (i, 0))] * 2
    names = ('qa', 'kv', 'kvb', 'gates', 'sza', 'szb', 'ckv', 'kpe', 'kmla', 'qmla', 'kc', 'vc')
    res = pl.pallas_call(
        functools.partial(_proj_kernel, with_cmp=with_cmp),
        grid=(n_tiles,), in_specs=in_specs, out_specs=out_specs, out_shape=out_shape,
        compiler_params=pltpu.CompilerParams(dimension_semantics=("arbitrary",), vmem_limit_bytes=56 * 2**20),
        name="in_proj",
    )(*args)
    return dict(zip(names, res))


def _out_kernel(x_ref, lig_ref, lib_ref, ma_ref, mb_ref, woa_ref, wob_ref, g_ref, b_ref, y_ref, *, alpha):
    h = _layernorm(x_ref[...], lig_ref[...], lib_ref[...])
    o = jnp.dot(ma_ref[...], woa_ref[...], preferred_element_type=jnp.float32)
    o = o + jnp.dot(mb_ref[...], wob_ref[...], preferred_element_type=jnp.float32)
    y_ref[...] = _layernorm(alpha * h + o, g_ref[...], b_ref[...])


def _out_project(x, mixed_a, mixed_b, wts, ln_in_g, ln_in_b, ln_g, ln_b, *, tm, alpha):
    m = x.shape[0]
    row = lambda w: pl.BlockSpec((tm, w), lambda i: (i, 0))
    full = lambda a: pl.BlockSpec(a.shape, lambda i: (0,) * a.ndim)
    vec = lambda v: v.reshape(1, -1)
    args = (x, vec(ln_in_g), vec(ln_in_b), mixed_a, mixed_b, wts['woa'], wts['wob'], vec(ln_g), vec(ln_b))
    in_specs = [row(D_MODEL), full(args[1]), full(args[2]), row(WIDTH_A), row(WIDTH_B),
                full(args[5]), full(args[6]), full(args[7]), full(args[8])]
    return pl.pallas_call(
        functools.partial(_out_kernel, alpha=alpha),
        grid=(m // tm,), in_specs=in_specs, out_specs=row(D_MODEL),
        out_shape=jax.ShapeDtypeStruct((m, D_MODEL), jnp.float32),
        compiler_params=pltpu.CompilerParams(dimension_semantics=("arbitrary",)),
        name="out_proj",
    )(*args)


def _softmax_update(s, mask, m_ref, l_ref, acc_ref, v, nt_v=False):
    m_old = m_ref[...]
    m_new = jnp.maximum(m_old, jnp.max(jnp.where(mask, s, NEG), axis=-1, keepdims=True))
    p = jnp.where(mask, jnp.exp(s - m_new), 0.0)
    alpha = jnp.exp(m_old - m_new)
    l_ref[...] = alpha * l_ref[...] + jnp.sum(p, axis=-1, keepdims=True)
    pb = p.astype(jnp.bfloat16)
    if nt_v:
        pv = lax.dot_general(pb, v, _NT, preferred_element_type=jnp.float32)
    else:
        pv = jnp.dot(pb, v, preferred_element_type=jnp.float32)
    acc_ref[...] = alpha * acc_ref[...] + pv
    m_ref[...] = m_new


def _softmax_init(m_ref, l_ref, acc_ref):
    m_ref[...] = jnp.full_like(m_ref, NEG)
    l_ref[...] = jnp.zeros_like(l_ref)
    acc_ref[...] = jnp.zeros_like(acc_ref)


def _softmax_finish(l_ref, acc_ref):
    l = l_ref[...]
    return acc_ref[...] / jnp.where(l > 0.0, l, 1.0)


def _top_blocks(score, lane, cur, n_valid):
    forced = (lane == 0) | (lane == cur) | (lane == cur - 1)
    sc = jnp.where(lane <= cur, jnp.where(forced, jnp.inf, score), -jnp.inf)
    sc = jnp.where(lane < n_valid, sc, -jnp.inf)
    w = sc.shape[1]
    cnt = jnp.zeros(sc.shape, jnp.int32)
    for r in range(1, n_valid):
        cnt += (pltpu.roll(sc, r, 1) >= sc).astype(jnp.int32)
        cnt += (pltpu.roll(sc, w - r, 1) > sc).astype(jnp.int32)
    return (cnt < N_SEL) & (lane <= cur) & (lane < n_valid)


def _nsa_prompt_kernel(qa_ref, kvb_ref, kc_ref, vc_ref, g_ref, sza_ref, out_ref,
                       m_ref, l_ref, acc_ref, *, tq, tk, seq):
    i = pl.program_id(1)
    q0 = i * tq
    rows = GROUP_A * tq
    lane = lax.broadcasted_iota(jnp.int32, (tq, LANES), 1)
    t_q = q0 + lax.broadcasted_iota(jnp.int32, (tq, LANES), 0)
    low = lane < HEAD_DIM_A
    n_sel_blocks = seq // SEL_BLOCK

    qa = qa_ref[...]
    kc = kc_ref[0].astype(jnp.bfloat16)
    vc = vc_ref[0].astype(jnp.bfloat16)
    n_half = seq // CMP_BLOCK // 2
    blk = jnp.where(lane < n_half, 2 * lane, 2 * (lane - n_half) + 1)
    cmp_mask1 = (blk * CMP_BLOCK + CMP_BLOCK - 1 <= t_q) & (lane < 2 * n_half)
    cmp_mask = jnp.concatenate([cmp_mask1] * GROUP_A, axis=0)

    kcol = lax.broadcasted_iota(jnp.int32, (tq, tk), 1)
    trow = q0 + lax.broadcasted_iota(jnp.int32, (tq, tk), 0)
    eblk = lax.broadcasted_iota(jnp.int32, (LANES, tk), 0) - lax.broadcasted_iota(jnp.int32, (LANES, tk), 1) // SEL_BLOCK

    kvs = range(N_KV_A)
    qk = [jnp.concatenate([jnp.where(low if k == 0 else ~low, qa[:, g * LANES:(g + 1) * LANES], 0)
                           for g in range(GROUP_A)], axis=0) for k in kvs]

    o_cmp, sel = [], []
    cur = t_q // SEL_BLOCK
    for k in kvs:
        s = lax.dot_general(qk[k], kc, _NT, preferred_element_type=jnp.float32)
        m = jnp.max(jnp.where(cmp_mask, s, NEG), axis=-1, keepdims=True)
        p = jnp.where(cmp_mask, jnp.exp(s - m), 0.0)
        l = jnp.sum(p, axis=-1, keepdims=True)
        p = p / jnp.where(l > 0.0, l, 1.0)
        o_cmp.append(jnp.dot(p.astype(jnp.bfloat16), vc, preferred_element_type=jnp.float32))
        imp = p[0:tq] + p[tq:2 * tq] + p[2 * tq:3 * tq] + p[3 * tq:4 * tq]
        score = imp + pltpu.roll(imp, LANES - n_half, 1)
        sel.append(lax.cond(q0 + tq > N_SEL * SEL_BLOCK,
                            lambda: _top_blocks(score, lane, cur, n_sel_blocks).astype(jnp.float32),
                            lambda: ((lane <= cur) & (lane < n_sel_blocks)).astype(jnp.float32)).astype(jnp.bfloat16))

    state = lambda k: (m_ref.at[k], l_ref.at[k], acc_ref.at[k])

    for k in kvs:
        _softmax_init(*state(k))
    n_tiles = (q0 + tq + tk - 1) // tk

    def sel_step(j, carry):
        k0 = pl.multiple_of(j * tk, tk)
        kt = kvb_ref[pl.ds(k0, tk), 0:KV_W]
        vt = kvb_ref[pl.ds(k0, tk), KV_W:2 * KV_W]
        expand = (eblk == j * (tk // SEL_BLOCK)).astype(jnp.bfloat16)
        causal = k0 + kcol <= trow
        for k in kvs:
            s = lax.dot_general(qk[k], kt, _NT, preferred_element_type=jnp.float32)
            mask1 = (jnp.dot(sel[k], expand, preferred_element_type=jnp.float32) > 0.5) & causal
            _softmax_update(s, jnp.concatenate([mask1] * GROUP_A, axis=0), *state(k), vt)
        return carry

    lax.fori_loop(0, n_tiles, sel_step, 0)
    o_sel = [_softmax_finish(*state(k)[1:]) for k in kvs]

    for k in kvs:
        _softmax_init(*state(k))
    j_lo = jnp.maximum(q0 - WINDOW + 1, 0) // tk

    def win_step(j, carry):
        k0 = pl.multiple_of(j * tk, tk)
        kt = kvb_ref[pl.ds(k0, tk), 2 * KV_W:3 * KV_W]
        vt = kvb_ref[pl.ds(k0, tk), 3 * KV_W:4 * KV_W]
        kp = k0 + kcol
        mask1 = (kp <= trow) & (kp > trow - WINDOW)
        mask = jnp.concatenate([mask1] * GROUP_A, axis=0)
        for k in kvs:
            s = lax.dot_general(qk[k], kt, _NT, preferred_element_type=jnp.float32)
            _softmax_update(s, mask, *state(k), vt)
        return carry

    lax.fori_loop(j_lo, n_tiles, win_step, 0)
    o_win = [_softmax_finish(*state(k)[1:]) for k in kvs]

    for g in range(GROUP_A):
        r = slice(g * tq, (g + 1) * tq)
        c = slice(g * LANES, (g + 1) * LANES)
        pick = lambda o: jnp.where(low, o[0][r], o[1][r])
        o = (g_ref[:, c] * pick(o_cmp) + g_ref[:, WIDTH_A + g * LANES:WIDTH_A + (g + 1) * LANES] * pick(o_sel)
             + g_ref[:, 2 * WIDTH_A + g * LANES:2 * WIDTH_A + (g + 1) * LANES] * pick(o_win))
        out_ref[:, c] = (o * sza_ref[:, c]).astype(jnp.bfloat16)


def _permute_cmp_blocks(x, b, n):
    x = x.reshape(b, n // 2, 2, LANES)
    x = jnp.concatenate([x[:, :, 0], x[:, :, 1]], axis=1)
    return jnp.pad(x, ((0, 0), (0, LANES - n), (0, 0)))


def _nsa_prompt(pr, b, seq, *, tq=128, tk=512):
    n_cmp = seq // CMP_BLOCK
    kc = _permute_cmp_blocks(pr['kc'], b, n_cmp)
    vc = _permute_cmp_blocks(pr['vc'], b, n_cmp)
    nq = seq // tq
    row = lambda w: pl.BlockSpec((tq, w), lambda bi, i: (bi * nq + i, 0))
    per_b = lambda w: pl.BlockSpec((seq, w), lambda bi, i: (bi, 0))
    cmp_spec = pl.BlockSpec((1, LANES, LANES), lambda bi, i: (bi, 0, 0))
    rows = GROUP_A * tq
    return pl.pallas_call(
        functools.partial(_nsa_prompt_kernel, tq=tq, tk=tk, seq=seq),
        grid=(b, nq),
        in_specs=[row(WIDTH_A), per_b(4 * KV_W), cmp_spec, cmp_spec, row(3 * WIDTH_A), row(WIDTH_A)],
        out_specs=row(WIDTH_A),
        out_shape=jax.ShapeDtypeStruct((b * seq, WIDTH_A), jnp.bfloat16),
        scratch_shapes=[pltpu.VMEM((N_KV_A, rows, 1), jnp.float32), pltpu.VMEM((N_KV_A, rows, 1), jnp.float32),
                        pltpu.VMEM((N_KV_A, rows, LANES), jnp.float32)],
        compiler_params=pltpu.CompilerParams(dimension_semantics=("arbitrary", "arbitrary")),
        name="nsa_prompt",
    )(pr['qa'], pr['kvb'], kc, vc, pr['gates'], pr['sza'])


def _mla_prompt_kernel(q_ref, k_ref, szb_ref, wuv_ref, out_ref, m_ref, l_ref, acc_ref, olat_ref, *, tq, tk, hg):
    i = pl.program_id(1)
    q0 = i * tq
    kcol = lax.broadcasted_iota(jnp.int32, (tq, tk), 1)
    trow = q0 + lax.broadcasted_iota(jnp.int32, (tq, tk), 0)
    n_tiles = (q0 + tq + tk - 1) // tk
    qw = 2 * LANES
    groups = range(N_HEADS_B // hg)
    q = [jnp.concatenate([q_ref[:, (grp * hg + h) * qw:(grp * hg + h + 1) * qw] for h in range(hg)], axis=0)
         for grp in groups]
    state = lambda grp: (m_ref.at[grp], l_ref.at[grp], acc_ref.at[grp])
    for grp in groups:
        _softmax_init(*state(grp))

    def step(j, carry):
        k0 = pl.multiple_of(j * tk, tk)
        kt = k_ref[pl.ds(k0, tk), :]
        mask = jnp.concatenate([k0 + kcol <= trow] * hg, axis=0)
        for grp in groups:
            s = lax.dot_general(q[grp], kt, _NT, preferred_element_type=jnp.float32) * MLA_SCALE
            _softmax_update(s, mask, *state(grp), kt[:, 0:KV_RANK])
        return carry

    lax.fori_loop(0, n_tiles, step, 0)
    for grp in groups:
        o = _softmax_finish(*state(grp)[1:])
        for h in range(hg):
            c = (grp * hg + h) * KV_RANK
            olat_ref[:, c:c + KV_RANK] = o[h * tq:(h + 1) * tq].astype(jnp.bfloat16)
    ob = jnp.dot(olat_ref[...], wuv_ref[...], preferred_element_type=jnp.float32)
    out_ref[...] = (ob * szb_ref[...]).astype(jnp.bfloat16)


def _mla_prompt(pr, wts, b, seq, *, tq=128, tk=512, hg=4):
    nq = seq // tq
    row = lambda w: pl.BlockSpec((tq, w), lambda bi, i: (bi * nq + i, 0))
    per_b = lambda w: pl.BlockSpec((seq, w), lambda bi, i: (bi, 0))
    rows = hg * tq
    ng = N_HEADS_B // hg
    return pl.pallas_call(
        functools.partial(_mla_prompt_kernel, tq=tq, tk=tk, hg=hg),
        grid=(b, nq),
        in_specs=[row(N_HEADS_B * 2 * LANES), per_b(2 * KV_RANK), row(WIDTH_B),
                  pl.BlockSpec(wts['wuv'].shape, lambda bi, i: (0, 0))],
        out_specs=row(WIDTH_B),
        out_shape=jax.ShapeDtypeStruct((b * seq, WIDTH_B), jnp.bfloat16),
        scratch_shapes=[pltpu.VMEM((ng, rows, 1), jnp.float32), pltpu.VMEM((ng, rows, 1), jnp.float32),
                        pltpu.VMEM((ng, rows, KV_RANK), jnp.float32), pltpu.VMEM((tq, N_HEADS_B * KV_RANK), jnp.bfloat16)],
        compiler_params=pltpu.CompilerParams(dimension_semantics=("arbitrary", "arbitrary")),
        name="mla_prompt",
    )(pr['qmla'], pr['kmla'], pr['szb'], wts['wuv'])


PAGES_PER_CHUNK = 64


def _paged_pipeline(pt_ref, pools, bufs, sem, n, compute):
    b, c = pl.program_id(0), pl.program_id(1)
    nb, nc = pl.num_programs(0), pl.num_programs(1)

    def copies(bb, chunk, slot, start):
        for k in range(len(pools)):
            for i in range(n):
                page = pt_ref[bb, chunk * n + i] if start else 0
                cp = pltpu.make_async_copy(pools[k].at[page], bufs[k].at[slot, i], sem.at[k, slot])
                if start:
                    cp.start()
                else:
                    cp.wait()

    @pl.when((b == 0) & (c == 0))
    def _():
        copies(b, 0, 0, True)

    copies(b, 2 * c + 1, 1, True)
    copies(b, 0, 0, False)
    compute(0, 2 * c)

    last_c = c == nc - 1
    next_b = jnp.where(last_c, b + 1, b)
    next_c = jnp.where(last_c, 0, c + 1)

    @pl.when(jnp.logical_not(last_c & (b == nb - 1)))
    def _():
        copies(next_b, 2 * next_c, 0, True)

    copies(b, 0, 1, False)
    compute(1, 2 * c + 1)


def _nsa_rows(qa, ds):
    low = lax.broadcasted_iota(jnp.int32, (ds, LANES), 1) < HEAD_DIM_A
    return jnp.concatenate([jnp.where(low if k == 0 else ~low, qa[:, g * LANES:(g + 1) * LANES], 0.0)
                            for k in range(N_KV_A) for g in range(GROUP_A)], axis=0)


def _cmp_pool_kernel(pt_ref, kpool, vpool, qa_ref, cc_ref, scl_ref, sch_ref, wpk_ref, wpv_ref,
                     ocmp_ref, score_ref, kbuf, vbuf, sem, kc_s, vc_s, *, n):
    blocks = PAGE_SIZE // CMP_BLOCK
    rows = n * blocks

    def compute(slot, chunk):
        r0 = pl.multiple_of(chunk * rows, rows)

        def means(buf):
            xt = jnp.concatenate([buf[slot, i] for i in range(n)], axis=1).T
            return jnp.mean(xt.reshape(rows, CMP_BLOCK, KV_W), axis=1).astype(jnp.bfloat16)

        kc = jnp.dot(means(kbuf), wpk_ref[...], preferred_element_type=jnp.float32)
        tab = lambda t: t[pl.ds(r0, rows), :]
        kc_s[pl.ds(r0, rows), :] = _rope(kc, tab(cc_ref), tab(scl_ref), tab(sch_ref), ROT_DIM_A // 2)
        vc_s[pl.ds(r0, rows), :] = jnp.dot(means(vbuf), wpv_ref[...], preferred_element_type=jnp.float32)

    _paged_pipeline(pt_ref, (kpool, vpool), (kbuf, vbuf), sem, n, compute)

    @pl.when(pl.program_id(1) == pl.num_programs(1) - 1)
    def _():
        ds = qa_ref.shape[1]
        qb = _nsa_rows(qa_ref[0], ds).astype(jnp.bfloat16)
        s = lax.dot_general(qb, kc_s[...].astype(jnp.bfloat16), _NT, preferred_element_type=jnp.float32)
        p = jnp.exp(s - jnp.max(s, axis=-1, keepdims=True))
        p = p / jnp.sum(p, axis=-1, keepdims=True)
        ocmp_ref[0] = jnp.dot(p.astype(jnp.bfloat16), vc_s[...].astype(jnp.bfloat16), preferred_element_type=jnp.float32)
        gd = GROUP_A * ds
        imp = jnp.concatenate([sum(p[k * gd + g * ds:k * gd + (g + 1) * ds] for g in range(GROUP_A))
                               for k in range(N_KV_A)], axis=0)
        w = imp.shape[1]
        even = lax.broadcasted_iota(jnp.int32, imp.shape, 1) % 2 == 0
        score_ref[0] = jnp.where(even, imp + pltpu.roll(imp, w - 1, 1), pltpu.roll(imp, 1, 1) + imp)


def _pool_t(cache):
    n_phys = cache.shape[1]
    return jnp.transpose(cache[0], (0, 2, 3, 1)).reshape(n_phys, KV_W, PAGE_SIZE)


def _cmp_pool(page_table, ps, k_pool_t, v_pool_t, wts, db, ds):
    n_pages = page_table.shape[1]
    n = PAGES_PER_CHUNK
    blocks = PAGE_SIZE // CMP_BLOCK
    n_cmp = n_pages * blocks
    cmp_pos = jnp.arange(n_cmp, dtype=jnp.int32) * CMP_BLOCK + (CMP_BLOCK - 1)
    tc = _rope_tables(cmp_pos, ROT_DIM_A, HEAD_DIM_A)
    f32 = jnp.float32
    const = lambda shape: pl.BlockSpec(shape, lambda b, c, pt: (0,) * len(shape))
    per_b = lambda r, w: pl.BlockSpec((1, r, w), lambda b, c, pt: (b, 0, 0))
    hbm = pl.BlockSpec(memory_space=pl.ANY)
    rows = N_KV_A * GROUP_A * ds
    page = (KV_W, PAGE_SIZE)
    return pl.pallas_call(
        functools.partial(_cmp_pool_kernel, n=n),
        grid_spec=pltpu.PrefetchScalarGridSpec(
            num_scalar_prefetch=1, grid=(db, n_pages // (2 * n)),
            in_specs=[hbm, hbm, per_b(ds, WIDTH_A)] + [const((n_cmp, LANES))] * 3 + [const((KV_W, KV_W))] * 2,
            out_specs=[per_b(rows, LANES), per_b(N_KV_A * ds, n_cmp)],
            scratch_shapes=[pltpu.VMEM((2, n) + page, f32), pltpu.VMEM((2, n) + page, f32),
                            pltpu.SemaphoreType.DMA((2, 2)),
                            pltpu.VMEM((n_cmp, LANES), f32), pltpu.VMEM((n_cmp, LANES), f32)]),
        out_shape=[jax.ShapeDtypeStruct((db, rows, LANES), f32), jax.ShapeDtypeStruct((db, N_KV_A * ds, n_cmp), f32)],
        compiler_params=pltpu.CompilerParams(dimension_semantics=("arbitrary", "arbitrary")),
        name="cmp_pool",
    )(page_table, k_pool_t, v_pool_t, ps['qa'].astype(f32).reshape(db, ds, WIDTH_A), *tc, wts['wpk'], wts['wpv'])


def _rank_kernel(sc_ref, sel_ref, sc_s, *, n_keep):
    nblk = sc_ref.shape[0]
    row = lax.broadcasted_iota(jnp.int32, sc_ref.shape, 0)
    sc = jnp.where((row == 0) | (row == nblk - 1), jnp.inf, sc_ref[...])
    sc_s[...] = sc

    def body(i, cnt):
        other = sc_s[pl.ds(i, 1), :]
        ge = (other >= sc).astype(jnp.int32)
        gt = (other > sc).astype(jnp.int32)
        return cnt + jnp.where(row > i, ge, gt)

    cnt = lax.fori_loop(0, nblk, body, jnp.zeros(sc_ref.shape, jnp.int32), unroll=8)
    sel_ref[...] = (cnt < n_keep).astype(jnp.float32)


def _rank_blocks(score_t, n_keep):
    nblk, rows = score_t.shape
    spec = pl.BlockSpec((nblk, LANES), lambda i: (0, i))
    return pl.pallas_call(
        functools.partial(_rank_kernel, n_keep=n_keep),
        grid=(rows // LANES,), in_specs=[spec], out_specs=spec,
        out_shape=jax.ShapeDtypeStruct((nblk, rows), jnp.float32),
        scratch_shapes=[pltpu.VMEM((nblk, LANES), jnp.float32)],
        compiler_params=pltpu.CompilerParams(dimension_semantics=("arbitrary",)),
        name="rank_blocks",
    )(score_t)


def _new_token_scores(q, k_new, t_row, ds):
    cols, masks = [], []
    for s in range(ds):
        cols.append(jnp.sum(q * k_new[s:s + 1, :], axis=-1, keepdims=True))
        masks.append(t_row >= s)
    return cols, masks


def _merge_new_tokens(m, l, acc, cols, masks, v_new):
    m_new = m
    for c, k in zip(cols, masks):
        m_new = jnp.maximum(m_new, jnp.where(k, c, NEG))
    alpha = jnp.exp(m - m_new)
    l, acc = alpha * l, alpha * acc
    for s, (c, k) in enumerate(zip(cols, masks)):
        p = jnp.where(k, jnp.exp(c - m_new), 0.0)
        l = l + p
        acc = acc + p.astype(jnp.bfloat16).astype(jnp.float32) * v_new[s:s + 1, :]
    return m_new, l, acc


def _nsa_sample_kernel(pt_ref, kpool, vpool, qa_ref, new_ref, ocmp_ref, sel_ref, exp_ref, wk_ref, wv_ref, g_ref, sza_ref,
                       out_ref, kbuf, vbuf, sem, m_ref, l_ref, acc_ref, owin_ref, *, n):
    c = pl.program_id(1)
    ds = qa_ref.shape[1]
    rows = N_KV_A * GROUP_A * ds
    low = lax.broadcasted_iota(jnp.int32, (ds, LANES), 1) < HEAD_DIM_A
    q = _nsa_rows(qa_ref[0], ds)
    qb = q.astype(jnp.bfloat16)
    t_row = lax.broadcasted_iota(jnp.int32, (rows, 1), 0) % ds
    new = new_ref[0]

    @pl.when(c == 0)
    def _():
        kw = wk_ref[0].astype(jnp.bfloat16)
        vw = wv_ref[0].astype(jnp.bfloat16)
        s = jnp.dot(qb, kw, preferred_element_type=jnp.float32)
        wb = s.shape[1]
        idx = lax.broadcasted_iota(jnp.int32, (rows, wb), 1)
        mask = idx > t_row + (wb - WINDOW)
        m = jnp.max(jnp.where(mask, s, NEG), axis=-1, keepdims=True)
        p = jnp.where(mask, jnp.exp(s - m), 0.0)
        l = jnp.sum(p, axis=-1, keepdims=True)
        acc = lax.dot_general(p.astype(jnp.bfloat16), vw, _NT, preferred_element_type=jnp.float32)
        cols, masks = _new_token_scores(q, new[:, 2 * KV_W:3 * KV_W], t_row, ds)
        m, l, acc = _merge_new_tokens(m, l, acc, cols, masks, new[:, 3 * KV_W:4 * KV_W])
        owin_ref[...] = acc / l

        cols, masks = _new_token_scores(q, new[:, 0:KV_W], t_row, ds)
        m0 = jnp.full((rows, 1), NEG, jnp.float32)
        m, l, acc = _merge_new_tokens(m0, jnp.zeros((rows, 1), jnp.float32), jnp.zeros((rows, LANES), jnp.float32),
                                      cols, masks, new[:, KV_W:2 * KV_W])
        m_ref[...], l_ref[...], acc_ref[...] = m, l, acc

    def compute(slot, chunk):
        sel = sel_ref[0, slot]
        sel_rows = jnp.concatenate([sel[k * ds:(k + 1) * ds] for k in range(N_KV_A) for _ in range(GROUP_A)], axis=0)
        mask = jnp.dot(sel_rows.astype(jnp.bfloat16), exp_ref[...], preferred_element_type=jnp.float32) > 0.5
        kt = jnp.concatenate([kbuf[slot, i].astype(jnp.bfloat16) for i in range(n)], axis=1)
        vt = jnp.concatenate([vbuf[slot, i].astype(jnp.bfloat16) for i in range(n)], axis=1)
        s = jnp.dot(qb, kt, preferred_element_type=jnp.float32)
        m_old = m_ref[...]
        m_new = jnp.maximum(m_old, jnp.max(jnp.where(mask, s, NEG), axis=-1, keepdims=True))
        p = jnp.where(mask, jnp.exp(s - m_new), 0.0)
        alpha = jnp.exp(m_old - m_new)
        l_ref[...] = alpha * l_ref[...] + jnp.sum(p, axis=-1, keepdims=True)
        pv = lax.dot_general(p.astype(jnp.bfloat16), vt, _NT, preferred_element_type=jnp.float32)
        acc_ref[...] = alpha * acc_ref[...] + pv
        m_ref[...] = m_new

    _paged_pipeline(pt_ref, (kpool, vpool), (kbuf, vbuf), sem, n, compute)

    @pl.when(c == pl.num_programs(1) - 1)
    def _():
        osel = acc_ref[...] / l_ref[...]
        gd = GROUP_A * ds
        for g in range(GROUP_A):
            pick = lambda o: jnp.where(low, o[g * ds:(g + 1) * ds], o[gd + g * ds:gd + (g + 1) * ds])
            cs = slice(g * LANES, (g + 1) * LANES)
            o = (g_ref[0, :, cs] * pick(ocmp_ref[0])
                 + g_ref[0, :, WIDTH_A + g * LANES:WIDTH_A + (g + 1) * LANES] * pick(osel)
                 + g_ref[0, :, 2 * WIDTH_A + g * LANES:2 * WIDTH_A + (g + 1) * LANES] * pick(owin_ref[...]))
            out_ref[0, :, cs] = o * sza_ref[0, :, cs]


def _select_past_blocks(score, db, ds, n):
    rows = score.shape[1]
    score_t = score[:, :, 0::2].reshape(db * rows, -1).T
    sel_t = _rank_blocks(score_t, N_SEL - 1)
    per_chunk = n * PAGE_SIZE // SEL_BLOCK
    sel = sel_t.T.reshape(db, rows, -1, per_chunk).transpose(0, 2, 1, 3)
    return jnp.pad(sel, ((0, 0), (0, 0), (0, 0), (0, LANES - per_chunk)))


def _nsa_sample(page_table, ps, ocmp, sel, k_pool_t, v_pool_t, win_k_t, win_v_t, db, ds):
    n = PAGES_PER_CHUNK
    n_pages = page_table.shape[1]
    wb = win_k_t.shape[2]
    f32 = jnp.float32
    per_b = lambda r, w: pl.BlockSpec((1, r, w), lambda b, c, pt: (b, 0, 0))
    hbm = pl.BlockSpec(memory_space=pl.ANY)
    rows = N_KV_A * GROUP_A * ds
    width = n * PAGE_SIZE
    expand = jnp.asarray(np.arange(LANES)[:, None] == np.arange(width)[None, :] // SEL_BLOCK, jnp.bfloat16)
    args = (ps['qa'].astype(f32).reshape(db, ds, WIDTH_A), ps['kvb'].astype(f32).reshape(db, ds, 4 * KV_W),
            ocmp, sel, expand, win_k_t, win_v_t, ps['gates'].reshape(db, ds, 3 * WIDTH_A), ps['sza'].reshape(db, ds, WIDTH_A))
    in_specs = [hbm, hbm, per_b(ds, WIDTH_A), per_b(ds, 4 * KV_W), per_b(rows, LANES),
                pl.BlockSpec((1, 2, N_KV_A * ds, LANES), lambda b, c, pt: (b, c, 0, 0)),
                pl.BlockSpec((LANES, width), lambda b, c, pt: (0, 0)),
                per_b(KV_W, wb), per_b(KV_W, wb), per_b(ds, 3 * WIDTH_A), per_b(ds, WIDTH_A)]
    page = (KV_W, PAGE_SIZE)
    return pl.pallas_call(
        functools.partial(_nsa_sample_kernel, n=n),
        grid_spec=pltpu.PrefetchScalarGridSpec(
            num_scalar_prefetch=1, grid=(db, n_pages // (2 * n)), in_specs=in_specs,
            out_specs=per_b(ds, WIDTH_A),
            scratch_shapes=[pltpu.VMEM((2, n) + page, f32), pltpu.VMEM((2, n) + page, f32),
                            pltpu.SemaphoreType.DMA((2, 2)),
                            pltpu.VMEM((rows, 1), f32), pltpu.VMEM((rows, 1), f32), pltpu.VMEM((rows, LANES), f32),
                            pltpu.VMEM((rows, LANES), f32)]),
        out_shape=jax.ShapeDtypeStruct((db, ds, WIDTH_A), f32),
        compiler_params=pltpu.CompilerParams(dimension_semantics=("arbitrary", "arbitrary")),
        name="nsa_sample",
    )(page_table, k_pool_t, v_pool_t, *args)


def _mla_sample_kernel(pt_ref, ckv_pool, kpe_pool, q_ref, new_ref, szb_ref, wuv_ref, out_ref,
                       ckv_buf, kpe_buf, sem, m_ref, l_ref, acc_ref, *, n):
    c = pl.program_id(1)
    ds = q_ref.shape[1]
    rows = N_HEADS_B * ds
    qw = 2 * LANES
    q = jnp.concatenate([q_ref[0, :, h * qw:(h + 1) * qw] for h in range(N_HEADS_B)], axis=0)
    q_abs = q[:, :KV_RANK].astype(jnp.bfloat16)
    q_pe = q[:, KV_RANK:].astype(jnp.bfloat16)
    t_row = lax.broadcasted_iota(jnp.int32, (rows, 1), 0) % ds

    @pl.when(c == 0)
    def _():
        new = new_ref[0]
        cols, masks = _new_token_scores(q, new, t_row, ds)
        cols = [x * MLA_SCALE for x in cols]
        m0 = jnp.full((rows, 1), NEG, jnp.float32)
        m, l, acc = _merge_new_tokens(m0, jnp.zeros((rows, 1), jnp.float32), jnp.zeros((rows, KV_RANK), jnp.float32),
                                      cols, masks, new[:, :KV_RANK])
        m_ref[...], l_ref[...], acc_ref[...] = m, l, acc

    pad = jnp.zeros((LANES - ROPE_DIM_B, n * PAGE_SIZE), jnp.float32)

    def compute(slot, chunk):
        ckv = jnp.concatenate([ckv_buf[slot, i].astype(jnp.bfloat16) for i in range(n)], axis=0)
        kpe = jnp.concatenate([kpe_buf[slot, i] for i in range(n)] , axis=1)
        kpe = jnp.concatenate([kpe, pad], axis=0).astype(jnp.bfloat16)
        s = (lax.dot_general(q_abs, ckv, _NT, preferred_element_type=jnp.float32)
             + jnp.dot(q_pe, kpe, preferred_element_type=jnp.float32)) * MLA_SCALE
        m_old = m_ref[...]
        m_new = jnp.maximum(m_old, jnp.max(s, axis=-1, keepdims=True))
        p = jnp.exp(s - m_new)
        alpha = jnp.exp(m_old - m_new)
        l_ref[...] = alpha * l_ref[...] + jnp.sum(p, axis=-1, keepdims=True)
        pv = jnp.dot(p.astype(jnp.bfloat16), ckv, preferred_element_type=jnp.float32)
        acc_ref[...] = alpha * acc_ref[...] + pv
        m_ref[...] = m_new

    _paged_pipeline(pt_ref, (ckv_pool, kpe_pool), (ckv_buf, kpe_buf), sem, n, compute)

    @pl.when(c == pl.num_programs(1) - 1)
    def _():
        o = acc_ref[...] / l_ref[...]
        wide = jnp.concatenate([o[h * ds:(h + 1) * ds] for h in range(N_HEADS_B)], axis=1)
        wide = jnp.concatenate([wide, jnp.zeros_like(wide)], axis=0).astype(jnp.bfloat16)
        ob = jnp.dot(wide, wuv_ref[...], preferred_element_type=jnp.float32)[:ds]
        out_ref[0] = ob * szb_ref[0]


def _mla_sample(page_table, ps, ckv_pool, kpe_pool_t, wts, db, ds):
    n = PAGES_PER_CHUNK
    n_pages = page_table.shape[1]
    f32 = jnp.float32
    per_b = lambda r, w: pl.BlockSpec((1, r, w), lambda b, c, pt: (b, 0, 0))
    hbm = pl.BlockSpec(memory_space=pl.ANY)
    rows = N_HEADS_B * ds
    args = (ps['qmla'].astype(f32).reshape(db, ds, N_HEADS_B * 2 * LANES), ps['kmla'].astype(f32).reshape(db, ds, 2 * KV_RANK),
            ps['szb'].reshape(db, ds, WIDTH_B), wts['wuv'])
    in_specs = [hbm, hbm, per_b(ds, N_HEADS_B * 2 * LANES), per_b(ds, 2 * KV_RANK), per_b(ds, WIDTH_B),
                pl.BlockSpec(wts['wuv'].shape, lambda b, c, pt: (0, 0))]
    return pl.pallas_call(
        functools.partial(_mla_sample_kernel, n=n),
        grid_spec=pltpu.PrefetchScalarGridSpec(
            num_scalar_prefetch=1, grid=(db, n_pages // (2 * n)), in_specs=in_specs,
            out_specs=per_b(ds, WIDTH_B),
            scratch_shapes=[pltpu.VMEM((2, n, PAGE_SIZE, KV_RANK), f32), pltpu.VMEM((2, n, ROPE_DIM_B, PAGE_SIZE), f32),
                            pltpu.SemaphoreType.DMA((2, 2)),
                            pltpu.VMEM((rows, 1), f32), pltpu.VMEM((rows, 1), f32), pltpu.VMEM((rows, KV_RANK), f32)]),
        out_shape=jax.ShapeDtypeStruct((db, ds, WIDTH_B), f32),
        compiler_params=pltpu.CompilerParams(dimension_semantics=("arbitrary", "arbitrary")),
        name="mla_sample",
    )(page_table, ckv_pool, kpe_pool_t, *args)


def _kv_heads(x, lead):
    return x.reshape(*lead, N_KV_A, HEAD_DIM_A)


def kernel(x_prompt, x_sample, cache_k_cmp, cache_v_cmp, cache_k_sel, cache_v_sel, cache_ckv, cache_kpe, state_win_k, state_win_v, page_table, ln_in_g, ln_in_b, w_in, w_phi_k, w_phi_v, q_norm_g, w_uq, kv_norm_g, w_uk, w_uv, w_o, ln_g, ln_b):
    depth = w_in.shape[0]
    assert depth == 1, "one layer only"
    alpha = (2.0 * depth) ** 0.25
    b, seq, _ = x_prompt.shape
    db, ds, _ = x_sample.shape
    past = page_table.shape[1] * PAGE_SIZE
    l = 0
    wts = _prep_weights(w_in[l], w_phi_k[l], w_phi_v[l], w_uq[l], w_uk[l], w_uv[l], w_o[l])

    xp = x_prompt.reshape(b * seq, D_MODEL)
    pos_p = jnp.arange(seq, dtype=jnp.int32)
    cmp_pos = jnp.arange(seq // CMP_BLOCK, dtype=jnp.int32) * CMP_BLOCK + (CMP_BLOCK - 1)
    pp = _project(xp, pos_p, cmp_pos, wts, ln_in_g, ln_in_b, q_norm_g[l], kv_norm_g[l], tm=256, with_cmp=True)
    mixed_a = _nsa_prompt(pp, b, seq)
    mixed_b = _mla_prompt(pp, wts, b, seq)
    y_p = _out_project(xp, mixed_a, mixed_b, wts, ln_in_g, ln_in_b, ln_g[l], ln_b[l], tm=256, alpha=alpha)

    kvp = pp['kv'].reshape(b, seq, 6, N_KV_A, HEAD_DIM_A)
    wl = min(WINDOW, seq)
    p_out = [kvp[None, :, :, i] for i in range(4)]
    p_ckv = pp['ckv'].reshape(1, b, seq, KV_RANK)
    p_kpe = pp['kpe'].reshape(b, seq, LANES)[None, :, :, :ROPE_DIM_B]
    p_wk, p_wv = kvp[None, :, seq - wl:, 4], kvp[None, :, seq - wl:, 5]

    xs = x_sample.reshape(db * ds, D_MODEL)
    tm_s = 256
    pos_s = jnp.tile(past + jnp.arange(ds, dtype=jnp.int32), tm_s // ds)
    ps = _project(xs, pos_s, jnp.zeros((tm_s // CMP_BLOCK,), jnp.int32), wts, ln_in_g, ln_in_b, q_norm_g[l], kv_norm_g[l],
                  tm=tm_s, with_cmp=False)
    ocmp_s, score_s = _cmp_pool(page_table, ps, _pool_t(cache_k_cmp), _pool_t(cache_v_cmp), wts, db, ds)
    sel_s = _select_past_blocks(score_s, db, ds, PAGES_PER_CHUNK)
    win_t = lambda st: jnp.transpose(st[l], (0, 2, 3, 1)).reshape(db, KV_W, st.shape[2])
    mixed_a_s = _nsa_sample(page_table, ps, ocmp_s, sel_s, _pool_t(cache_k_sel), _pool_t(cache_v_sel),
                            win_t(state_win_k), win_t(state_win_v), db, ds)
    kpe_pool_t = jnp.transpose(cache_kpe[l], (0, 2, 1))
    mixed_b_s = _mla_sample(page_table, ps, cache_ckv[l], kpe_pool_t, wts, db, ds)
    y_s = _out_project(xs, mixed_a_s.reshape(db * ds, WIDTH_A).astype(jnp.bfloat16),
                       mixed_b_s.reshape(db * ds, WIDTH_B).astype(jnp.bfloat16),
                       wts, ln_in_g, ln_in_b, ln_g[l], ln_b[l], tm=tm_s, alpha=alpha)

    kvs = ps['kv'].reshape(db, ds, 6, N_KV_A, HEAD_DIM_A)
    s_out = [kvs[None, :, :, i] for i in range(4)]
    s_ckv = ps['ckv'].reshape(1, db, ds, KV_RANK)
    s_kpe = ps['kpe'].reshape(db, ds, LANES)[None, :, :, :ROPE_DIM_B]
    wb = state_win_k.shape[2]
    s_wk = jnp.concatenate([state_win_k[l], kvs[:, :, 4]], axis=1)[None, :, -wb:]
    s_wv = jnp.concatenate([state_win_v[l], kvs[:, :, 5]], axis=1)[None, :, -wb:]
    return (y_p.reshape(b, seq, D_MODEL), y_s.reshape(db, ds, D_MODEL), *p_out, p_ckv, p_kpe, p_wk, p_wv,
            *s_out, s_ckv, s_kpe, s_wk, s_wv)
```
